```python
import jax, jax.numpy as jnp
from jax import lax
import numpy as np

D_MODEL = 2048
BATCH = 4
SEQ = 2048
DEPTH = 4

D_FF = 5632
SSM_WIDTH = 1024
SSM_GROUP = 16
SSM_GROUPS = SSM_WIDTH // SSM_GROUP
SSM_STATE = 64
DT_MIN = 1e-3
DT_MAX = 1e-1
GDN_HEADS = 8
GDN_HEAD_DIM = 128
GDN_WIDTH = GDN_HEADS * GDN_HEAD_DIM
CONV_K = 4
CHUNK = 64
IN_SIZES = (SSM_WIDTH, GDN_WIDTH, GDN_WIDTH, GDN_WIDTH, GDN_WIDTH, GDN_HEADS, GDN_HEADS, D_MODEL, D_MODEL)
IN_COLS = sum(IN_SIZES)
LN_EPS = 1e-5
RMS_EPS = 1e-6
L2_EPS = 1e-6

kernel_name = 'hybrid_s5_gdn_macaron_deepnorm'


def layer_norm(x, g, b):
    xf = x.astype(jnp.float32)
    mu = jnp.mean(xf, axis=-1, keepdims=True)
    var = jnp.mean(jnp.square(xf - mu), axis=-1, keepdims=True)
    y = (xf - mu) * lax.rsqrt(var + LN_EPS) * g.astype(jnp.float32) + b.astype(jnp.float32)
    return y.astype(x.dtype)


def swiglu_ffn(x, w_gu, w_down):
    gate, up = jnp.split(x @ w_gu, 2, axis=-1)
    return (jax.nn.silu(gate) * up) @ w_down


def cmul(ar, ai, br, bi):
    return ar * br - ai * bi, ar * bi + ai * br


def s5_branch(u, a_re, a_im, log_dt, b_re, b_im, c_re, c_im, d_skip, glu_w, glu_b):
    f32 = jnp.float32
    bsz, seq, _ = u.shape
    ug = u.astype(f32).reshape(bsz, seq, SSM_GROUPS, SSM_GROUP)
    dt = jnp.exp(log_dt.astype(f32))[:, None]
    lr, li = a_re.astype(f32), a_im.astype(f32)
    mag = jnp.exp(lr * dt)
    lbar_r, lbar_i = mag * jnp.cos(li * dt), mag * jnp.sin(li * dt)
    den = lr * lr + li * li
    zr, zi = cmul(lbar_r - 1.0, lbar_i, lr / den, -li / den)
    bbar_r, bbar_i = cmul(zr[:, :, None], zi[:, :, None], b_re.astype(f32), b_im.astype(f32))
    bu_r = jnp.einsum('blgh,gph->blgp', ug, bbar_r)
    bu_i = jnp.einsum('blgh,gph->blgp', ug, bbar_i)
    a_r = jnp.broadcast_to(lbar_r, (1, seq) + lbar_r.shape)
    a_i = jnp.broadcast_to(lbar_i, (1, seq) + lbar_i.shape)

    def combine(e_early, e_late):
        a1r, a1i, b1r, b1i = e_early
        a2r, a2i, b2r, b2i = e_late
        ar, ai = cmul(a2r, a2i, a1r, a1i)
        br, bi = cmul(a2r, a2i, b1r, b1i)
        return (ar, ai, br + b2r, bi + b2i)

    _, _, s_r, s_i = lax.associative_scan(combine, (a_r, a_i, bu_r, bu_i), axis=1)
    y = (jnp.einsum('ghp,blgp->blgh', c_re.astype(f32), s_r)
         - jnp.einsum('ghp,blgp->blgh', c_im.astype(f32), s_i)
         + d_skip.astype(f32) * ug)
    y = y.reshape(bsz, seq, SSM_WIDTH).astype(u.dtype)
    y = jax.nn.gelu(y)
    return y * jax.nn.sigmoid(y @ glu_w + glu_b)


def causal_dwconv(x, w):
    return lax.conv_general_dilated(
        x, w[:, None, :], window_strides=(1,), padding=[(CONV_K - 1, 0)],
        dimension_numbers=('NWC', 'WIO', 'NWC'), feature_group_count=x.shape[-1])


def l2norm(t):
    return t * lax.rsqrt(jnp.sum(t * t, axis=-1, keepdims=True) + L2_EPS)


def gated_deltanet_branch(q, k, v, z, beta_logit, a_in, conv_w, a_log, dt_bias, norm_w):
    f32 = jnp.float32
    bsz, seq, _ = q.shape
    n_chunks = seq // CHUNK
    qkv = jax.nn.silu(causal_dwconv(jnp.concatenate([q, k, v], axis=-1), conv_w)).astype(f32)
    q, k, v = jnp.split(qkv, 3, axis=-1)

    def heads(t):
        return t.reshape(bsz, n_chunks, CHUNK, GDN_HEADS, GDN_HEAD_DIM).transpose(0, 3, 1, 2, 4)

    def head_scalars(t):
        return t.reshape(bsz, n_chunks, CHUNK, GDN_HEADS).transpose(0, 3, 1, 2)

    q = l2norm(heads(q)) * (GDN_HEAD_DIM ** -0.5)
    k = l2norm(heads(k))
    v = heads(v)
    beta = head_scalars(jax.nn.sigmoid(beta_logit.astype(f32)))
    g = -jnp.exp(a_log.astype(f32)) * jax.nn.softplus(a_in.astype(f32) + dt_bias.astype(f32))
    gcum = jnp.cumsum(head_scalars(g), axis=-1)
    idx = jnp.arange(CHUNK)
    causal = idx[:, None] >= idx[None, :]
    strict = idx[:, None] > idx[None, :]
    decay = jnp.exp(jnp.where(causal, gcum[..., :, None] - gcum[..., None, :], -jnp.inf))
    k_beta = k * beta[..., None]
    lower = jnp.where(strict, jnp.einsum('bhncd,bhnsd->bhncs', k_beta, k) * decay, 0.0)
    rhs = jnp.concatenate([v * beta[..., None], k_beta * jnp.exp(gcum)[..., None]], axis=-1)
    sol = lax.linalg.triangular_solve(lower + jnp.eye(CHUNK, dtype=f32), rhs,
                                      left_side=True, lower=True, unit_diagonal=True)
    u_val, w_key = jnp.split(sol, 2, axis=-1)
    attn_intra = jnp.einsum('bhncd,bhnsd->bhncs', q, k) * decay
    q_dec = q * jnp.exp(gcum)[..., None]
    k_dec = k * jnp.exp(gcum[..., -1:] - gcum)[..., None]
    g_last = jnp.exp(gcum[..., -1])

    def chunk_step(state, xs):
        u_c, w_c, a_c, qd_c, kd_c, gl_c = xs
        v_new = u_c - jnp.einsum('bhcd,bhde->bhce', w_c, state)
        out = (jnp.einsum('bhcd,bhde->bhce', qd_c, state)
               + jnp.einsum('bhcs,bhse->bhce', a_c, v_new))
        state = state * gl_c[..., None, None] + jnp.einsum('bhcd,bhce->bhde', kd_c, v_new)
        return state, out

    xs = tuple(jnp.moveaxis(t, 2, 0) for t in (u_val, w_key, attn_intra, q_dec, k_dec, g_last))
    state0 = jnp.zeros((bsz, GDN_HEADS, GDN_HEAD_DIM, GDN_HEAD_DIM), f32)
    _, o = lax.scan(chunk_step, state0, xs)
    o = o.transpose(1, 0, 3, 2, 4).reshape(bsz, seq, GDN_HEADS, GDN_HEAD_DIM)
    o = o * lax.rsqrt(jnp.mean(o * o, axis=-1, keepdims=True) + RMS_EPS) * norm_w.astype(f32)
    o = o * jax.nn.silu(z.astype(f32).reshape(bsz, seq, GDN_HEADS, GDN_HEAD_DIM))
    return o.reshape(bsz, seq, GDN_WIDTH).astype(q.dtype if q.dtype != f32 else z.dtype)


def hybrid_mixer(h, w_in, conv_w, ssm_a_re, ssm_a_im, ssm_log_dt, ssm_b_re, ssm_b_im, ssm_c_re,
                 ssm_c_im, ssm_d, glu_w, glu_b, gdn_a_log, gdn_dt_bias, gdn_norm_w,
                 w_br_ssm, w_br_gdn, w_out):
    offsets = np.cumsum(IN_SIZES)[:-1].tolist()
    u, q, k, v, z, beta_logit, a_in, gate_ssm, gate_gdn = jnp.split(h @ w_in, offsets, axis=-1)
    y_ssm = s5_branch(u, ssm_a_re, ssm_a_im, ssm_log_dt, ssm_b_re, ssm_b_im,
                      ssm_c_re, ssm_c_im, ssm_d, glu_w, glu_b)
    y_gdn = gated_deltanet_branch(q, k, v, z, beta_logit, a_in, conv_w,
                                  gdn_a_log, gdn_dt_bias, gdn_norm_w)
    merged = (jax.nn.sigmoid(gate_ssm) * (y_ssm @ w_br_ssm)
              + jax.nn.sigmoid(gate_gdn) * (y_gdn @ w_br_gdn))
    return merged @ w_out


def setup_inputs(seed: int = 0) -> dict:
    key = jax.random.key(seed)
    ks = jax.random.split(key, 32)
    f32 = jnp.float32
    L = DEPTH
    dn_beta = (8.0 * DEPTH) ** -0.25

    def nrm(k, shape, scale):
        return scale * jax.random.normal(k, shape, f32)

    lo, hi = float(np.log(DT_MIN)), float(np.log(DT_MAX))
    gdn_dt = jnp.exp(jax.random.uniform(ks[20], (L, GDN_HEADS), f32, lo, hi))
    return {
        'x': nrm(ks[0], (BATCH, SEQ, D_MODEL), 1.0),
        'ffn1_w_gu': nrm(ks[1], (L, D_MODEL, 2 * D_FF), D_MODEL ** -0.5),
        'ffn1_w_down': nrm(ks[2], (L, D_FF, D_MODEL), dn_beta * D_FF ** -0.5),
        'ln1_g': 1.0 + nrm(ks[3], (L, D_MODEL), 0.02),
        'ln1_b': nrm(ks[4], (L, D_MODEL), 0.02),
        'w_in': nrm(ks[5], (L, D_MODEL, IN_COLS), D_MODEL ** -0.5),
        'conv_w': nrm(ks[6], (L, CONV_K, 3 * GDN_WIDTH), CONV_K ** -0.5),
        'ssm_a_re': -0.5 + nrm(ks[7], (L, SSM_GROUPS, SSM_STATE), 0.02),
        'ssm_a_im': jnp.pi * jnp.arange(SSM_STATE, dtype=f32) + nrm(ks[8], (L, SSM_GROUPS, SSM_STATE), 0.02),
        'ssm_log_dt': jax.random.uniform(ks[9], (L, SSM_GROUPS), f32, lo, hi),
        'ssm_b_re': nrm(ks[10], (L, SSM_GROUPS, SSM_STATE, SSM_GROUP), (2 * SSM_GROUP) ** -0.5),
        'ssm_b_im': nrm(ks[11], (L, SSM_GROUPS, SSM_STATE, SSM_GROUP), (2 * SSM_GROUP) ** -0.5),
        'ssm_c_re': nrm(ks[12], (L, SSM_GROUPS, SSM_GROUP, SSM_STATE), (2 * SSM_STATE) ** -0.5),
        'ssm_c_im': nrm(ks[13], (L, SSM_GROUPS, SSM_GROUP, SSM_STATE), (2 * SSM_STATE) ** -0.5),
        'ssm_d': nrm(ks[14], (L, SSM_GROUPS, SSM_GROUP), 1.0),
        'glu_w': nrm(ks[15], (L, SSM_WIDTH, SSM_WIDTH), SSM_WIDTH ** -0.5),
        'glu_b': nrm(ks[16], (L, SSM_WIDTH), 0.02),
        'gdn_a_log': jnp.log(jax.random.uniform(ks[17], (L, GDN_HEADS), f32, 1.0, 16.0)),
        'gdn_dt_bias': gdn_dt + jnp.log(-jnp.expm1(-gdn_dt)),
        'gdn_norm_w': 1.0 + nrm(ks[18], (L, GDN_HEAD_DIM), 0.02),
        'w_br_ssm': nrm(ks[19], (L, SSM_WIDTH, D_MODEL), SSM_WIDTH ** -0.5),
        'w_br_gdn': nrm(ks[21], (L, GDN_WIDTH, D_MODEL), GDN_WIDTH ** -0.5),
        'w_out': nrm(ks[22], (L, D_MODEL, D_MODEL), dn_beta * D_MODEL ** -0.5),
        'ln2_g': 1.0 + nrm(ks[23], (L, D_MODEL), 0.02),
        'ln2_b': nrm(ks[24], (L, D_MODEL), 0.02),
        'ffn2_w_gu': nrm(ks[25], (L, D_MODEL, 2 * D_FF), D_MODEL ** -0.5),
        'ffn2_w_down': nrm(ks[26], (L, D_FF, D_MODEL), dn_beta * D_FF ** -0.5),
        'ln3_g': 1.0 + nrm(ks[27], (L, D_MODEL), 0.02),
        'ln3_b': nrm(ks[28], (L, D_MODEL), 0.02),
    }


def reference(x, ffn1_w_gu, ffn1_w_down, ln1_g, ln1_b, w_in, conv_w, ssm_a_re, ssm_a_im,
              ssm_log_dt, ssm_b_re, ssm_b_im, ssm_c_re, ssm_c_im, ssm_d, glu_w, glu_b,
              gdn_a_log, gdn_dt_bias, gdn_norm_w, w_br_ssm, w_br_gdn, w_out, ln2_g, ln2_b,
              ffn2_w_gu, ffn2_w_down, ln3_g, ln3_b):
    alpha = (2.0 * DEPTH) ** 0.25
    for l in range(DEPTH):
        x = layer_norm(alpha * x + 0.5 * swiglu_ffn(x, ffn1_w_gu[l], ffn1_w_down[l]), ln1_g[l], ln1_b[l])
        mix = hybrid_mixer(x, w_in[l], conv_w[l], ssm_a_re[l], ssm_a_im[l], ssm_log_dt[l],
                           ssm_b_re[l], ssm_b_im[l], ssm_c_re[l], ssm_c_im[l], ssm_d[l],
                           glu_w[l], glu_b[l], gdn_a_log[l], gdn_dt_bias[l], gdn_norm_w[l],
                           w_br_ssm[l], w_br_gdn[l], w_out[l])
        x = layer_norm(alpha * x + mix, ln2_g[l], ln2_b[l])
        x = layer_norm(alpha * x + 0.5 * swiglu_ffn(x, ffn2_w_gu[l], ffn2_w_down[l]), ln3_g[l], ln3_b[l])
    return x
```

```python
import functools
import math

import jax
import jax.numpy as jnp
from jax import lax
from jax.experimental import pallas as pl
from jax.experimental.pallas import tpu as pltpu

F32 = jnp.float32
BF16 = jnp.bfloat16
ACT = BF16
HIGHEST = lax.Precision.HIGHEST

LN_EPS = 1e-5
RMS_EPS = 1e-6
L2_EPS = 1e-6
CONV_K = 4
GDN_CHUNK = 64
GDN_GROUP = 256
S5_TC = 256
S5_GROUPS_PER_BLOCK = 16
LANES = 128
SUBLANES = 8
VMEM_LIMIT = 56 * 1024 * 1024


def _cparams(sem):
    return pltpu.CompilerParams(dimension_semantics=sem, vmem_limit_bytes=VMEM_LIMIT)


def _pick(n, cands):
    for c in cands:
        if n % c == 0:
            return c
    raise ValueError(f"no tile in {cands} divides {n}")


def _layer_norm(y, g, b):
    mu = jnp.mean(y, axis=-1, keepdims=True)
    yc = y - mu
    var = jnp.mean(yc * yc, axis=-1, keepdims=True)
    return yc * lax.rsqrt(var + LN_EPS) * g + b


def _silu(x):
    return x * jax.nn.sigmoid(x)


def _cmul(ar, ai, br, bi):
    return ar * br - ai * bi, ar * bi + ai * br


def _bdot(a, b):
    return jnp.dot(a.astype(BF16), b.astype(BF16), preferred_element_type=F32)


def _hdot(a, b):
    return jnp.dot(a, b, precision=HIGHEST, preferred_element_type=F32)


def _ffn_body(l_ref, x_ref, wg_ref, wu_ref, wd_ref, g_ref, b_ref, o_ref, xb_ref, acc_ref, *, nf, alpha):
    j = pl.program_id(1)

    @pl.when(j == 0)
    def _():
        xb_ref[...] = x_ref[...].astype(BF16)
        acc_ref[...] = jnp.zeros_like(acc_ref)

    xb = xb_ref[...]
    gate = jnp.dot(xb, wg_ref[...], preferred_element_type=F32)
    up = jnp.dot(xb, wu_ref[...], preferred_element_type=F32)
    h = (_silu(gate) * up).astype(BF16)
    acc_ref[...] += jnp.dot(h, wd_ref[...], preferred_element_type=F32)

    @pl.when(j == nf - 1)
    def _():
        y = alpha * x_ref[...] + 0.5 * acc_ref[...]
        o_ref[...] = _layer_norm(y, g_ref[...], b_ref[...])


def _ffn(l, x, wgu, wd, g, b, *, alpha):
    m, d = x.shape
    f = wd.shape[1]
    tm = _pick(m, (512, 256, 128))
    tf = _pick(f, (512, 256, 128))
    nf = f // tf
    return pl.pallas_call(
        functools.partial(_ffn_body, nf=nf, alpha=alpha),
        grid_spec=pltpu.PrefetchScalarGridSpec(
            num_scalar_prefetch=1,
            grid=(m // tm, nf),
            in_specs=[
                pl.BlockSpec((tm, d), lambda i, j, l: (i, 0)),
                pl.BlockSpec((None, d, tf), lambda i, j, l: (l[0], 0, j)),
                pl.BlockSpec((None, d, tf), lambda i, j, l: (l[0], 0, j + nf)),
                pl.BlockSpec((None, tf, d), lambda i, j, l: (l[0], j, 0)),
                pl.BlockSpec((None, 1, d), lambda i, j, l: (l[0], 0, 0)),
                pl.BlockSpec((None, 1, d), lambda i, j, l: (l[0], 0, 0)),
            ],
            out_specs=pl.BlockSpec((tm, d), lambda i, j, l: (i, 0)),
            scratch_shapes=[pltpu.VMEM((tm, d), BF16), pltpu.VMEM((tm, d), F32)],
        ),
        out_shape=jax.ShapeDtypeStruct((m, d), F32),
        compiler_params=_cparams(("parallel", "arbitrary")),
        name="ffn",
    )(l, x, wgu, wgu, wd, g, b)


def _inproj_body(l_ref, x_ref, w_ref, ws_ref, o_ref, os_ref, xb_ref):
    j = pl.program_id(1)

    @pl.when(j == 0)
    def _():
        xb_ref[...] = x_ref[...].astype(BF16)
        os_ref[...] = jnp.dot(xb_ref[...], ws_ref[...], preferred_element_type=F32)

    o_ref[...] = jnp.dot(xb_ref[...], w_ref[...], preferred_element_type=F32).astype(o_ref.dtype)


def _inproj(l, x, w_main, w_small):
    m, d = x.shape
    n = w_main.shape[2]
    tm = _pick(m, (1024, 512, 256, 128))
    tn = _pick(n, (1024, 512, 256, 128))
    return pl.pallas_call(
        _inproj_body,
        grid_spec=pltpu.PrefetchScalarGridSpec(
            num_scalar_prefetch=1,
            grid=(m // tm, n // tn),
            in_specs=[
                pl.BlockSpec((tm, d), lambda i, j, l: (i, 0)),
                pl.BlockSpec((None, d, tn), lambda i, j, l: (l[0], 0, j)),
                pl.BlockSpec((None, d, LANES), lambda i, j, l: (l[0], 0, 0)),
            ],
            out_specs=[
                pl.BlockSpec((tm, tn), lambda i, j, l: (i, j)),
                pl.BlockSpec((tm, LANES), lambda i, j, l: (i, 0)),
            ],
            scratch_shapes=[pltpu.VMEM((tm, d), BF16)],
        ),
        out_shape=[jax.ShapeDtypeStruct((m, n), ACT), jax.ShapeDtypeStruct((m, LANES), F32)],
        compiler_params=_cparams(("parallel", "arbitrary")),
        name="inproj",
    )(l, x, w_main, w_small)


def _s5_prep_body(lr_ref, li_ref, ldt_ref, bre_ref, bim_ref, bbar_ref, coef_ref, *, nb, ns_blk):
    lr = lr_ref[...]
    li = li_ref[...]
    ns = lr.shape[-1]
    dt = jnp.exp(ldt_ref[...])
    mag = jnp.exp(lr * dt)
    lbr = mag * jnp.cos(li * dt)
    lbi = mag * jnp.sin(li * dt)
    den = lr * lr + li * li
    zr, zi = _cmul(lbr - 1.0, lbi, lr / den, -li / den)
    for cb in range(nb):
        sl = slice(cb * ns_blk, (cb + 1) * ns_blk)
        zr_c, zi_c = zr[:, sl], zi[:, sl]
        br, bi = bre_ref[cb], bim_ref[cb]
        bbar_ref[cb, :, 0:ns_blk] = (zr_c * br - zi_c * bi).astype(BF16)
        bbar_ref[cb, :, ns_blk:2 * ns_blk] = (zr_c * bi + zi_c * br).astype(BF16)
    pw = [(lbr, lbi)]
    for _ in range(SUBLANES - 1):
        pw.append(_cmul(pw[-1][0], pw[-1][1], lbr, lbi))
    row = lax.broadcasted_iota(jnp.int32, (SUBLANES, ns), 0)

    def bc(v):
        return jnp.broadcast_to(v, (SUBLANES, ns))

    for idx, dd in enumerate((1, 2, 4)):
        coef_ref[2 * idx] = jnp.where(row >= dd, bc(pw[dd - 1][0]), 0.0)
        coef_ref[2 * idx + 1] = jnp.where(row >= dd, bc(pw[dd - 1][1]), 0.0)
    pr = jnp.zeros((SUBLANES, ns), F32)
    pi = jnp.zeros((SUBLANES, ns), F32)
    for r in range(SUBLANES):
        pr = jnp.where(row == r, bc(pw[r][0]), pr)
        pi = jnp.where(row == r, bc(pw[r][1]), pi)
    coef_ref[6] = pr
    coef_ref[7] = pi


def _s5_prep(lr, li, ldt, bre_bd, bim_bd):
    depth, nb, kin, ns_blk = bre_bd.shape
    ns = lr.shape[-1]
    return pl.pallas_call(
        functools.partial(_s5_prep_body, nb=nb, ns_blk=ns_blk),
        grid=(depth,),
        in_specs=[
            pl.BlockSpec((None, 1, ns), lambda l: (l, 0, 0)),
            pl.BlockSpec((None, 1, ns), lambda l: (l, 0, 0)),
            pl.BlockSpec((None, 1, ns), lambda l: (l, 0, 0)),
            pl.BlockSpec((None, nb, kin, ns_blk), lambda l: (l, 0, 0, 0)),
            pl.BlockSpec((None, nb, kin, ns_blk), lambda l: (l, 0, 0, 0)),
        ],
        out_specs=[
            pl.BlockSpec((None, nb, kin, 2 * ns_blk), lambda l: (l, 0, 0, 0)),
            pl.BlockSpec((None, 8, SUBLANES, ns), lambda l: (l, 0, 0, 0)),
        ],
        out_shape=[
            jax.ShapeDtypeStruct((depth, nb, kin, 2 * ns_blk), BF16),
            jax.ShapeDtypeStruct((depth, 8, SUBLANES, ns), F32),
        ],
        compiler_params=_cparams(("arbitrary",)),
        name="s5_prep",
    )(lr, li, ldt, bre_bd, bim_bd)


def _s5_body(l_ref, u_ref, bbar_ref, coef_ref, cre_ref, cim_ref, d_ref, gw_ref, gb_ref, o_ref,
             bu_ref, s_ref, y_ref, carry_ref, *, nb, kin, ns_blk, tc):
    t = pl.program_id(1)

    @pl.when(t == 0)
    def _():
        carry_ref[...] = jnp.zeros_like(carry_ref)

    for cb in range(nb):
        ucb = u_ref[:, cb * kin:(cb + 1) * kin]
        bu_ref[...] = jnp.dot(ucb.astype(BF16), bbar_ref[cb], preferred_element_type=F32)
        c0 = cb * ns_blk

        def tile(i, carry, c0=c0):
            cr, ci = carry
            r0 = pl.multiple_of(i * SUBLANES, SUBLANES)
            xr = bu_ref[pl.ds(r0, SUBLANES), 0:ns_blk]
            xi = bu_ref[pl.ds(r0, SUBLANES), ns_blk:2 * ns_blk]
            for idx, dd in enumerate((1, 2, 4)):
                ar = coef_ref[2 * idx, :, c0:c0 + ns_blk]
                ai = coef_ref[2 * idx + 1, :, c0:c0 + ns_blk]
                sr = pltpu.roll(xr, dd, 0)
                si = pltpu.roll(xi, dd, 0)
                xr, xi = xr + ar * sr - ai * si, xi + ar * si + ai * sr
            pr = coef_ref[6, :, c0:c0 + ns_blk]
            pi = coef_ref[7, :, c0:c0 + ns_blk]
            xr, xi = xr + pr * cr - pi * ci, xi + pr * ci + pi * cr
            s_ref[pl.ds(r0, SUBLANES), 0:ns_blk] = xr
            s_ref[pl.ds(r0, SUBLANES), ns_blk:2 * ns_blk] = xi
            cr = jnp.broadcast_to(xr[SUBLANES - 1:SUBLANES, :], (SUBLANES, ns_blk))
            ci = jnp.broadcast_to(xi[SUBLANES - 1:SUBLANES, :], (SUBLANES, ns_blk))
            return cr, ci

        cr, ci = lax.fori_loop(0, tc // SUBLANES, tile, (carry_ref[cb, 0], carry_ref[cb, 1]))
        carry_ref[cb, 0] = cr
        carry_ref[cb, 1] = ci
        y = (jnp.dot(s_ref[:, 0:ns_blk].astype(BF16), cre_ref[cb], preferred_element_type=F32)
             - jnp.dot(s_ref[:, ns_blk:2 * ns_blk].astype(BF16), cim_ref[cb], preferred_element_type=F32)
             + d_ref[:, cb * kin:(cb + 1) * kin] * ucb.astype(F32))
        y_ref[:, cb * kin:(cb + 1) * kin] = y

    y = jax.nn.gelu(y_ref[...], approximate=True)
    gate = jnp.dot(y.astype(BF16), gw_ref[...], preferred_element_type=F32) + gb_ref[...]
    o_ref[...] = (y * jax.nn.sigmoid(gate)).astype(o_ref.dtype)


def _s5(l, proj, bbar, coef, cre_bd, cim_bd, dskip, glu_w, glu_b, *, batch, seq):
    depth, nb, kin, ns2 = bbar.shape
    ns_blk = ns2 // 2
    width = nb * kin
    ns = nb * ns_blk
    tc = _pick(seq, (S5_TC, 128, 64))
    nt = seq // tc
    m = batch * seq
    return pl.pallas_call(
        functools.partial(_s5_body, nb=nb, kin=kin, ns_blk=ns_blk, tc=tc),
        grid_spec=pltpu.PrefetchScalarGridSpec(
            num_scalar_prefetch=1,
            grid=(batch, nt),
            in_specs=[
                pl.BlockSpec((tc, width), lambda b, t, l: (b * nt + t, 0)),
                pl.BlockSpec((None, nb, kin, ns2), lambda b, t, l: (l[0], 0, 0, 0)),
                pl.BlockSpec((None, 8, SUBLANES, ns), lambda b, t, l: (l[0], 0, 0, 0)),
                pl.BlockSpec((None, nb, ns_blk, kin), lambda b, t, l: (l[0], 0, 0, 0)),
                pl.BlockSpec((None, nb, ns_blk, kin), lambda b, t, l: (l[0], 0, 0, 0)),
                pl.BlockSpec((None, 1, width), lambda b, t, l: (l[0], 0, 0)),
                pl.BlockSpec((None, width, width), lambda b, t, l: (l[0], 0, 0)),
                pl.BlockSpec((None, 1, width), lambda b, t, l: (l[0], 0, 0)),
            ],
            out_specs=pl.BlockSpec((tc, width), lambda b, t, l: (b * nt + t, 0)),
            scratch_shapes=[
                pltpu.VMEM((tc, ns2), F32),
                pltpu.VMEM((tc, ns2), F32),
                pltpu.VMEM((tc, width), F32),
                pltpu.VMEM((nb, 2, SUBLANES, ns_blk), F32),
            ],
        ),
        out_shape=jax.ShapeDtypeStruct((m, width), ACT),
        compiler_params=_cparams(("parallel", "arbitrary")),
        name="s5",
    )(l, proj, bbar, coef, cre_bd, cim_bd, dskip, glu_w, glu_b)


def _gdn_body(l_ref, q_ref, k_ref, v_ref, z_ref, sm_ref, cwq_ref, cwk_ref, cwv_ref, alog_ref, dtb_ref,
              nw_ref, o_ref, xpad, qs, ks, vs, gs, bs, us, ws, ats, *, seq, heads, hd):
    h = pl.program_id(1)
    rb = GDN_GROUP
    ch = GDN_CHUNK
    pad = SUBLANES

    xpad[0:pad, :] = jnp.zeros((pad, hd), F32)
    for src, cw, dst, mode in ((q_ref, cwq_ref, qs, "q"), (k_ref, cwk_ref, ks, "k"), (v_ref, cwv_ref, vs, "v")):
        xpad[pad:pad + seq, :] = src[...].astype(F32)
        w = cw[...]
        for r0 in range(0, seq, rb):
            acc = w[0:1, :] * xpad[r0 + pad - 3:r0 + pad - 3 + rb, :]
            for j in range(1, CONV_K):
                acc = acc + w[j:j + 1, :] * xpad[r0 + pad - 3 + j:r0 + pad - 3 + j + rb, :]
            y = _silu(acc)
            if mode != "v":
                y = y * lax.rsqrt(jnp.sum(y * y, axis=-1, keepdims=True) + L2_EPS)
            if mode == "q":
                y = y * (hd ** -0.5)
            dst[r0:r0 + rb, :] = y

    lane = lax.broadcasted_iota(jnp.int32, (rb, LANES), 1)
    neg_a = -jnp.exp(alog_ref[...])
    dtb = dtb_ref[...]
    for r0 in range(0, seq, rb):
        s = sm_ref[r0:r0 + rb, :]
        xa = s + dtb
        sp = jnp.maximum(xa, 0.0) + jnp.log(1.0 + jnp.exp(-jnp.abs(xa)))
        gall = neg_a * sp
        ball = jax.nn.sigmoid(s)
        gcol = jnp.sum(jnp.where(lane == heads + h, gall, 0.0), axis=-1, keepdims=True)
        bcol = jnp.sum(jnp.where(lane == h, ball, 0.0), axis=-1, keepdims=True)
        gs[r0:r0 + rb, :] = jnp.broadcast_to(gcol, (rb, hd))
        bs[r0:r0 + rb, :] = jnp.broadcast_to(bcol, (rb, hd))

    row = lax.broadcasted_iota(jnp.int32, (rb, rb), 0)
    col = lax.broadcasted_iota(jnp.int32, (rb, rb), 1)

    def same(blk):
        return (row // blk) == (col // blk)

    lower = row >= col
    strict = row > col
    same_ch = same(ch)
    causal_ch = same_ch & lower
    strict_ch = same_ch & strict
    tril_f = jnp.where(causal_ch, 1.0, 0.0).astype(F32)
    ones_f = jnp.where(same_ch, 1.0, 0.0).astype(F32)
    eye_f = jnp.where(row == col, 1.0, 0.0).astype(F32)
    m16 = jnp.where(same(16), 1.0, 0.0).astype(F32)
    m32 = jnp.where(same(32), 1.0, 0.0).astype(F32)
    reps = rb // hd

    def group(gi, carry):
        r0 = pl.multiple_of(gi * rb, rb)
        q = qs[pl.ds(r0, rb), :]
        k = ks[pl.ds(r0, rb), :]
        v = vs[pl.ds(r0, rb), :]
        gb = gs[pl.ds(r0, rb), :]
        bb = bs[pl.ds(r0, rb), :]
        gc = _hdot(tril_f, gb)
        gl = _hdot(ones_f, gb)
        eg = jnp.exp(gc)
        gcw = jnp.concatenate([gc] * reps, axis=1)
        diff = gcw - gcw.T
        e = jnp.exp(jnp.where(causal_ch, diff, 0.0))
        d_incl = jnp.where(causal_ch, e, 0.0)
        d_strict = jnp.where(strict_ch, e, 0.0)
        kb = k * bb
        kbf = k.astype(BF16)
        kk = lax.dot_general(kb.astype(BF16), kbf, (((1,), (1,)), ((), ())), preferred_element_type=F32)
        a = kk * d_strict
        x = -(a * m16)
        tinv = eye_f + x
        p = x
        for _ in range(3):
            p = _hdot(p, p)
            tinv = tinv + _hdot(tinv, p)
        a1 = a * (m32 - m16)
        tinv = tinv - _hdot(tinv, _hdot(a1, tinv))
        a2 = a * (ones_f - m32)
        tinv = tinv - _hdot(tinv, _hdot(a2, tinv))
        rhs = jnp.concatenate([v * bb, kb * eg], axis=1)
        sol = _bdot(tinv, rhs)
        us[pl.ds(r0, rb), :] = sol[:, 0:hd]
        ws[pl.ds(r0, rb), :] = sol[:, hd:2 * hd]
        qk = lax.dot_general(q.astype(BF16), kbf, (((1,), (1,)), ((), ())), preferred_element_type=F32)
        attn = qk * d_incl
        for c in range(rb // ch):
            ats[pl.ds(r0 + c * ch, ch), :] = attn[c * ch:(c + 1) * ch, c * ch:(c + 1) * ch]
        qs[pl.ds(r0, rb), :] = q * eg
        ks[pl.ds(r0, rb), :] = k * jnp.exp(gl - gc)
        gs[pl.ds(r0, rb), :] = jnp.exp(gl)
        return carry

    lax.fori_loop(0, seq // rb, group, 0)

    nw = nw_ref[...]

    def step(n, state):
        r0 = pl.multiple_of(n * ch, ch)
        u = us[pl.ds(r0, ch), :]
        w = ws[pl.ds(r0, ch), :]
        qd = qs[pl.ds(r0, ch), :]
        kd = ks[pl.ds(r0, ch), :]
        at = ats[pl.ds(r0, ch), :]
        gl = gs[pl.ds(r0, SUBLANES), :][0:1, :]
        wq = jnp.concatenate([w, qd], axis=0)
        r = _bdot(wq, state)
        vn = u - r[0:ch, :]
        vnb = vn.astype(BF16)
        out = r[ch:2 * ch, :] + jnp.dot(at.astype(BF16), vnb, preferred_element_type=F32)
        state = state * gl + lax.dot_general(kd.astype(BF16), vnb, (((0,), (0,)), ((), ())),
                                             preferred_element_type=F32)
        o = out * lax.rsqrt(jnp.mean(out * out, axis=-1, keepdims=True) + RMS_EPS) * nw
        zz = z_ref[pl.ds(r0, ch), :].astype(F32)
        o_ref[pl.ds(r0, ch), :] = (o * _silu(zz)).astype(o_ref.dtype)
        return state

    lax.fori_loop(0, seq // ch, step, jnp.zeros((hd, hd), F32))


def _gdn(l, proj, small, conv_w, alog, dtb, norm_w, *, batch, seq, heads, hd, col0):
    m = batch * seq
    cb = col0

    def act_spec(off):
        return pl.BlockSpec((seq, hd), lambda b, h, l, off=off: (b, off + h))

    def cw_spec(off):
        return pl.BlockSpec((None, CONV_K, hd), lambda b, h, l, off=off: (l[0], 0, off + h))

    row_spec = pl.BlockSpec((None, 1, LANES), lambda b, h, l: (l[0], 0, 0))
    return pl.pallas_call(
        functools.partial(_gdn_body, seq=seq, heads=heads, hd=hd),
        grid_spec=pltpu.PrefetchScalarGridSpec(
            num_scalar_prefetch=1,
            grid=(batch, heads),
            in_specs=[
                act_spec(cb), act_spec(cb + heads), act_spec(cb + 2 * heads), act_spec(cb + 3 * heads),
                pl.BlockSpec((seq, LANES), lambda b, h, l: (b, 0)),
                cw_spec(0), cw_spec(heads), cw_spec(2 * heads),
                row_spec, row_spec,
                pl.BlockSpec((None, 1, hd), lambda b, h, l: (l[0], 0, 0)),
            ],
            out_specs=pl.BlockSpec((seq, hd), lambda b, h, l: (b, h)),
            scratch_shapes=(
                [pltpu.VMEM((seq + SUBLANES, hd), F32)]
                + [pltpu.VMEM((seq, hd), F32) for _ in range(7)]
                + [pltpu.VMEM((seq, GDN_CHUNK), F32)]
            ),
        ),
        out_shape=jax.ShapeDtypeStruct((m, heads * hd), ACT),
        compiler_params=_cparams(("parallel", "parallel")),
        name="gdn",
    )(l, proj, proj, proj, proj, small, conv_w, conv_w, conv_w, alog, dtb, norm_w)


def _merge_body(l_ref, x_ref, ys_ref, yg_ref, gs0_ref, gs1_ref, gd0_ref, gd1_ref, wbs_ref, wbg_ref, wo_ref,
                g_ref, b_ref, o_ref, *, alpha):
    ys = ys_ref[...].astype(BF16)
    yg = yg_ref[...].astype(BF16)
    hw = wo_ref.shape[0] // 2
    acc = None
    for half, (gsr, gdr) in enumerate(((gs0_ref, gd0_ref), (gs1_ref, gd1_ref))):
        sl = slice(half * hw, (half + 1) * hw)
        mix = (jax.nn.sigmoid(gsr[...].astype(F32)) * jnp.dot(ys, wbs_ref[:, sl], preferred_element_type=F32)
               + jax.nn.sigmoid(gdr[...].astype(F32)) * jnp.dot(yg, wbg_ref[:, sl], preferred_element_type=F32))
        part = jnp.dot(mix.astype(BF16), wo_ref[sl, :], preferred_element_type=F32)
        acc = part if acc is None else acc + part
    y = alpha * x_ref[...] + acc
    o_ref[...] = _layer_norm(y, g_ref[...], b_ref[...])


def _merge(l, x, y_ssm, y_gdn, proj, wbs, wbg, wo, g, b, *, gate_col0, alpha):
    m, d = x.shape
    hw = d // 2
    ws = y_ssm.shape[1]
    wg = y_gdn.shape[1]
    tm = _pick(m, (256, 128))

    def gate_spec(off):
        return pl.BlockSpec((tm, hw), lambda i, l, off=off: (i, gate_col0 + off))

    return pl.pallas_call(
        functools.partial(_merge_body, alpha=alpha),
        grid_spec=pltpu.PrefetchScalarGridSpec(
            num_scalar_prefetch=1,
            grid=(m // tm,),
            in_specs=[
                pl.BlockSpec((tm, d), lambda i, l: (i, 0)),
                pl.BlockSpec((tm, ws), lambda i, l: (i, 0)),
                pl.BlockSpec((tm, wg), lambda i, l: (i, 0)),
                gate_spec(0), gate_spec(1), gate_spec(2), gate_spec(3),
                pl.BlockSpec((None, ws, d), lambda i, l: (l[0], 0, 0)),
                pl.BlockSpec((None, wg, d), lambda i, l: (l[0], 0, 0)),
                pl.BlockSpec((None, d, d), lambda i, l: (l[0], 0, 0)),
                pl.BlockSpec((None, 1, d), lambda i, l: (l[0], 0, 0)),
                pl.BlockSpec((None, 1, d), lambda i, l: (l[0], 0, 0)),
            ],
            out_specs=pl.BlockSpec((tm, d), lambda i, l: (i, 0)),
        ),
        out_shape=jax.ShapeDtypeStruct((m, d), F32),
        compiler_params=_cparams(("parallel",)),
        name="merge",
    )(l, x, y_ssm, y_gdn, proj, proj, proj, proj, wbs, wbg, wo, g, b)


def _block_diag(t, gpb):
    depth, g, r, c = t.shape
    nb = g // gpb
    t5 = t.reshape(depth, nb, gpb, r, c)
    eye = jnp.eye(gpb, dtype=t.dtype)
    out = jnp.einsum("lngrc,gk->lngrkc", t5, eye)
    return out.reshape(depth, nb, gpb * r, gpb * c)


def kernel(x, ffn1_w_gu, ffn1_w_down, ln1_g, ln1_b, w_in, conv_w, ssm_a_re, ssm_a_im, ssm_log_dt, ssm_b_re,
           ssm_b_im, ssm_c_re, ssm_c_im, ssm_d, glu_w, glu_b, gdn_a_log, gdn_dt_bias, gdn_norm_w, w_br_ssm,
           w_br_gdn, w_out, ln2_g, ln2_b, ffn2_w_gu, ffn2_w_down, ln3_g, ln3_b):
    batch, seq, d = x.shape
    depth = ffn1_w_gu.shape[0]
    alpha = (2.0 * depth) ** 0.25
    groups, state, sgrp = ssm_b_re.shape[1:]
    ssm_width = groups * sgrp
    heads = gdn_a_log.shape[1]
    hd = gdn_norm_w.shape[1]
    gdn_width = heads * hd
    n_act = ssm_width + 4 * gdn_width
    n_small = 2 * heads

    def row3(t):
        return t.reshape(depth, 1, -1).astype(F32)

    wgu1, wd1 = ffn1_w_gu.astype(BF16), ffn1_w_down.astype(BF16)
    wgu2, wd2 = ffn2_w_gu.astype(BF16), ffn2_w_down.astype(BF16)
    w_main = jnp.concatenate([w_in[:, :, :n_act], w_in[:, :, n_act + n_small:]], axis=2).astype(BF16)
    w_small = jnp.pad(w_in[:, :, n_act:n_act + n_small], ((0, 0), (0, 0), (0, LANES - n_small))).astype(BF16)
    wbs, wbg, wo = w_br_ssm.astype(BF16), w_br_gdn.astype(BF16), w_out.astype(BF16)
    gluw = glu_w.astype(BF16)

    gpb = S5_GROUPS_PER_BLOCK
    bre_bd = _block_diag(jnp.swapaxes(ssm_b_re, 2, 3), gpb)
    bim_bd = _block_diag(jnp.swapaxes(ssm_b_im, 2, 3), gpb)
    cre_bd = _block_diag(jnp.swapaxes(ssm_c_re, 2, 3), gpb).astype(BF16)
    cim_bd = _block_diag(jnp.swapaxes(ssm_c_im, 2, 3), gpb).astype(BF16)
    lr = row3(ssm_a_re)
    li = row3(ssm_a_im)
    ldt = row3(jnp.repeat(ssm_log_dt, state, axis=1))
    bbar, coef = _s5_prep(lr, li, ldt, bre_bd, bim_bd)

    alog = jnp.pad(gdn_a_log, ((0, 0), (heads, LANES - 2 * heads))).reshape(depth, 1, LANES)
    dtb = jnp.pad(gdn_dt_bias, ((0, 0), (heads, LANES - 2 * heads))).reshape(depth, 1, LANES)

    def layer(xc, li_):
        l = jnp.reshape(li_, (1,)).astype(jnp.int32)
        xc = _ffn(l, xc, wgu1, wd1, row3(ln1_g), row3(ln1_b), alpha=alpha)
        proj, small = _inproj(l, xc, w_main, w_small)
        y_ssm = _s5(l, proj, bbar, coef, cre_bd, cim_bd, row3(ssm_d), gluw, row3(glu_b), batch=batch, seq=seq)
        y_gdn = _gdn(l, proj, small, conv_w, alog, dtb, row3(gdn_norm_w), batch=batch, seq=seq, heads=heads,
                     hd=hd, col0=ssm_width // hd)
        xc = _merge(l, xc, y_ssm, y_gdn, proj, wbs, wbg, wo, row3(ln2_g), row3(ln2_b),
                    gate_col0=n_act // (d // 2), alpha=alpha)
        xc = _ffn(l, xc, wgu2, wd2, row3(ln3_g), row3(ln3_b), alpha=alpha)
        return xc, None

    out, _ = lax.scan(layer, x.reshape(batch * seq, d), jnp.arange(depth, dtype=jnp.int32))
    return out.reshape(batch, seq, d)
```

```python
import functools

import jax
import jax.numpy as jnp
from jax import lax
from jax.experimental import pallas as pl
from jax.experimental.pallas import tpu as pltpu

F32 = jnp.float32
BF16 = jnp.bfloat16
ACT = BF16

LN_EPS = 1e-5
RMS_EPS = 1e-6
L2_EPS = 1e-6
CONV_K = 4
GDN_CHUNK = 64
GDN_GROUP = 128
GDN_INTERLEAVE = 16
GDN_HEADS_PER_STEP = 4
S5_TC = 128
S5_GROUPS_PER_BLOCK = 16
LANES = 128
SUBLANES = 8
VMEM_LIMIT = 56 * 1024 * 1024


def _cparams(sem):
    return pltpu.CompilerParams(dimension_semantics=sem, vmem_limit_bytes=VMEM_LIMIT)


def _pick(n, cands):
    for c in cands:
        if n % c == 0:
            return c
    raise ValueError(f"no tile in {cands} divides {n}")


def _layer_norm(y, g, b):
    mu = jnp.mean(y, axis=-1, keepdims=True)
    yc = y - mu
    var = jnp.mean(yc * yc, axis=-1, keepdims=True)
    return yc * lax.rsqrt(var + LN_EPS) * g + b


def _silu(x):
    return x * jax.nn.sigmoid(x)


def _cmul(ar, ai, br, bi):
    return ar * br - ai * bi, ar * bi + ai * br


def _bdot(a, b):
    return jnp.dot(a.astype(BF16), b.astype(BF16), preferred_element_type=F32)


def _dot01(m01, x):
    x1 = x.astype(BF16)
    r1 = x - x1.astype(F32)
    x2 = r1.astype(BF16)
    x3 = (r1 - x2.astype(F32)).astype(BF16)
    return (jnp.dot(m01, x1, preferred_element_type=F32) + jnp.dot(m01, x2, preferred_element_type=F32)
            + jnp.dot(m01, x3, preferred_element_type=F32))


def _ffn_body(l_ref, x_ref, wg_ref, wu_ref, wd_ref, g_ref, b_ref, o_ref, xb_ref, *, nf, alpha):
    j = pl.program_id(1)

    @pl.when(j == 0)
    def _():
        xb_ref[...] = x_ref[...].astype(BF16)
        o_ref[...] = jnp.zeros_like(o_ref)

    xb = xb_ref[...]
    gate = jnp.dot(xb, wg_ref[...].astype(BF16), preferred_element_type=F32)
    up = jnp.dot(xb, wu_ref[...].astype(BF16), preferred_element_type=F32)
    h = (_silu(gate) * up).astype(BF16)
    o_ref[...] += jnp.dot(h, wd_ref[...].astype(BF16), preferred_element_type=F32)

    @pl.when(j == nf - 1)
    def _():
        y = alpha * x_ref[...] + 0.5 * o_ref[...]
        o_ref[...] = _layer_norm(y, g_ref[...], b_ref[...])


def _ffn(l, x, wgu, wd, g, b, *, alpha):
    m, d = x.shape
    f = wd.shape[1]
    tm = _pick(m, (1024, 512, 256, 128))
    tf = _pick(f, (256, 128))
    nf = f // tf
    return pl.pallas_call(
        functools.partial(_ffn_body, nf=nf, alpha=alpha),
        grid_spec=pltpu.PrefetchScalarGridSpec(
            num_scalar_prefetch=1,
            grid=(m // tm, nf),
            in_specs=[
                pl.BlockSpec((tm, d), lambda i, j, l: (i, 0), pipeline_mode=pl.Buffered(1)),
                pl.BlockSpec((None, d, tf), lambda i, j, l: (l[0], 0, j)),
                pl.BlockSpec((None, d, tf), lambda i, j, l: (l[0], 0, j + nf)),
                pl.BlockSpec((None, tf, d), lambda i, j, l: (l[0], j, 0)),
                pl.BlockSpec((None, 1, d), lambda i, j, l: (l[0], 0, 0)),
                pl.BlockSpec((None, 1, d), lambda i, j, l: (l[0], 0, 0)),
            ],
            out_specs=pl.BlockSpec((tm, d), lambda i, j, l: (i, 0)),
            scratch_shapes=[pltpu.VMEM((tm, d), BF16)],
        ),
        out_shape=jax.ShapeDtypeStruct((m, d), F32),
        compiler_params=_cparams(("parallel", "arbitrary")),
        name="ffn",
    )(l, x, wgu, wgu, wd, g, b)


def _inproj_body(l_ref, x_ref, wa_ref, wg_ref, ws_ref, o_ref, os_ref, xb_ref, *, na):
    j = pl.program_id(1)

    @pl.when(j == 0)
    def _():
        xb_ref[...] = x_ref[...].astype(BF16)
        os_ref[...] = jnp.dot(xb_ref[...], ws_ref[...], preferred_element_type=F32)

    @pl.when(j < na)
    def _():
        o_ref[...] = jnp.dot(xb_ref[...], wa_ref[...], preferred_element_type=F32).astype(o_ref.dtype)

    @pl.when(j >= na)
    def _():
        o_ref[...] = jnp.dot(xb_ref[...], wg_ref[...], preferred_element_type=F32).astype(o_ref.dtype)


def _inproj(l, x, w_act, w_gate, w_small):
    m, d = x.shape
    n_a, n_g = w_act.shape[2], w_gate.shape[2]
    n = n_a + n_g
    tm = _pick(m, (1024, 512, 256, 128))
    tn = next(c for c in (1024, 512, 256, 128) if n_a % c == 0 and n_g % c == 0)
    na = n_a // tn
    return pl.pallas_call(
        functools.partial(_inproj_body, na=na),
        grid_spec=pltpu.PrefetchScalarGridSpec(
            num_scalar_prefetch=1,
            grid=(m // tm, n // tn),
            in_specs=[
                pl.BlockSpec((tm, d), lambda i, j, l: (i, 0)),
                pl.BlockSpec((None, d, tn), lambda i, j, l: (l[0], 0, jnp.minimum(j, na - 1))),
                pl.BlockSpec((None, d, tn), lambda i, j, l: (l[0], 0, jnp.maximum(j - na, 0))),
                pl.BlockSpec((None, d, LANES), lambda i, j, l: (l[0], 0, 0)),
            ],
            out_specs=[
                pl.BlockSpec((tm, tn), lambda i, j, l: (i, j)),
                pl.BlockSpec((tm, LANES), lambda i, j, l: (i, 0)),
            ],
            scratch_shapes=[pltpu.VMEM((tm, d), BF16)],
        ),
        out_shape=[jax.ShapeDtypeStruct((m, n), ACT), jax.ShapeDtypeStruct((m, LANES), F32)],
        compiler_params=_cparams(("parallel", "arbitrary")),
        name="inproj",
    )(l, x, w_act, w_gate, w_small)


def _s5_discretise(lr, li, ldt):
    dt = jnp.exp(ldt)
    mag = jnp.exp(lr * dt)
    lbr = mag * jnp.cos(li * dt)
    lbi = mag * jnp.sin(li * dt)
    den = lr * lr + li * li
    zr, zi = _cmul(lbr - 1.0, lbi, lr / den, -li / den)
    return lbr, lbi, zr, zi


def _s5_prep_body(lr_ref, li_ref, ldt_ref, lrf_ref, lif_ref, ldtf_ref, bre_ref, bim_ref, cre_ref, cim_ref,
                  bbar_ref, creb_ref, cimb_ref, lam_ref, *, nb, ns_blk, batch):
    _, _, zr, zi = _s5_discretise(lr_ref[...], li_ref[...], ldt_ref[...])
    br, bi = bre_ref[...], bim_ref[...]
    bbar_ref[:, 0:ns_blk] = (zr * br - zi * bi).astype(BF16)
    bbar_ref[:, ns_blk:2 * ns_blk] = (zr * bi + zi * br).astype(BF16)
    creb_ref[...] = cre_ref[...].astype(BF16)
    cimb_ref[...] = cim_ref[...].astype(BF16)

    @pl.when(pl.program_id(1) == 0)
    def _():
        spb = SUBLANES // batch
        row = lax.broadcasted_iota(jnp.int32, (SUBLANES, ns_blk), 0)
        for ps in range(nb // spb):
            tiles = [jnp.zeros((SUBLANES, ns_blk), F32), jnp.zeros((SUBLANES, ns_blk), F32)]
            for ci in range(spb):
                cb = ps * spb + ci
                lb = _s5_discretise(lrf_ref[cb], lif_ref[cb], ldtf_ref[cb])[0:2]
                for part in range(2):
                    blk = jnp.broadcast_to(lb[part], (SUBLANES, ns_blk))
                    tiles[part] = jnp.where(row // batch == ci, blk, tiles[part])
            lam_ref[ps, 0] = tiles[0]
            lam_ref[ps, 1] = tiles[1]


def _s5_prep(lr, li, ldt, bre_bd, bim_bd, cre_bd, cim_bd, *, batch):
    depth, nb, kin, ns_blk = bre_bd.shape
    assert SUBLANES % batch == 0 and nb % (SUBLANES // batch) == 0
    npass = nb // (SUBLANES // batch)
    blk_row = pl.BlockSpec((None, None, 1, ns_blk), lambda l, c: (l, c, 0, 0))
    all_rows = pl.BlockSpec((None, nb, 1, ns_blk), lambda l, c: (l, 0, 0, 0))
    b_spec = pl.BlockSpec((None, None, kin, ns_blk), lambda l, c: (l, c, 0, 0))
    c_spec = pl.BlockSpec((None, None, ns_blk, kin), lambda l, c: (l, c, 0, 0))
    return pl.pallas_call(
        functools.partial(_s5_prep_body, nb=nb, ns_blk=ns_blk, batch=batch),
        grid=(depth, nb),
        in_specs=[blk_row, blk_row, blk_row, all_rows, all_rows, all_rows, b_spec, b_spec, c_spec, c_spec],
        out_specs=[
            pl.BlockSpec((None, None, kin, 2 * ns_blk), lambda l, c: (l, c, 0, 0)),
            c_spec, c_spec,
            pl.BlockSpec((None, npass, 2, SUBLANES, ns_blk), lambda l, c: (l, 0, 0, 0, 0)),
        ],
        out_shape=[
            jax.ShapeDtypeStruct((depth, nb, kin, 2 * ns_blk), BF16),
            jax.ShapeDtypeStruct(cre_bd.shape, BF16),
            jax.ShapeDtypeStruct(cim_bd.shape, BF16),
            jax.ShapeDtypeStruct((depth, npass, 2, SUBLANES, ns_blk), F32),
        ],
        compiler_params=_cparams(("arbitrary", "arbitrary")),
        name="s5_prep",
    )(lr, li, ldt, lr, li, ldt, bre_bd, bim_bd, cre_bd, cim_bd)


def _s5_body(l_ref, u_ref, bbar_ref, lam_ref, cre_ref, cim_ref, d_ref, gw_ref, gb_ref, o_ref,
             buf, y_ref, carry_ref, *, nb, kin, ns_blk, tc, batch):
    t = pl.program_id(0)
    width = nb * kin
    pitch = tc + SUBLANES
    nsl = ns_blk // LANES
    spb = SUBLANES // batch

    @pl.when(t == 0)
    def _():
        carry_ref[...] = jnp.zeros_like(carry_ref)

    u2 = u_ref[...].reshape(batch * tc, width)
    for ps in range(nb // spb):
        for ci in range(spb):
            cb = ps * spb + ci
            bu = jnp.dot(u2[:, cb * kin:(cb + 1) * kin].astype(BF16), bbar_ref[cb], preferred_element_type=F32)
            for b in range(batch):
                r0 = (ci * batch + b) * pitch
                for j in range(2 * nsl):
                    buf[j, r0:r0 + tc, :] = bu[b * tc:(b + 1) * tc, j * LANES:(j + 1) * LANES]

        def scan_step(i, carry, ps=ps):
            new = []
            for j in range(nsl):
                ar = lam_ref[ps, 0, :, j * LANES:(j + 1) * LANES]
                ai = lam_ref[ps, 1, :, j * LANES:(j + 1) * LANES]
                sr, si = carry[j], carry[nsl + j]
                xr = buf[j, pl.ds(i, SUBLANES, stride=pitch), :]
                xi = buf[nsl + j, pl.ds(i, SUBLANES, stride=pitch), :]
                nr = ar * sr - ai * si + xr
                ni = ar * si + ai * sr + xi
                buf[j, pl.ds(i, SUBLANES, stride=pitch), :] = nr
                buf[nsl + j, pl.ds(i, SUBLANES, stride=pitch), :] = ni
                new.append((nr, ni))
            return tuple(x[0] for x in new) + tuple(x[1] for x in new)

        init = tuple(carry_ref[ps, j] for j in range(2 * nsl))
        fin = lax.fori_loop(0, tc, scan_step, init, unroll=4)
        for j in range(2 * nsl):
            carry_ref[ps, j] = fin[j]

        for ci in range(spb):
            cb = ps * spb + ci

            def gather(part, ci=ci):
                return jnp.concatenate(
                    [jnp.concatenate([buf[part * nsl + j, (ci * batch + b) * pitch:(ci * batch + b) * pitch + tc, :]
                                      for j in range(nsl)], axis=1) for b in range(batch)], axis=0).astype(BF16)

            y = (jnp.dot(gather(0), cre_ref[cb], preferred_element_type=F32)
                 - jnp.dot(gather(1), cim_ref[cb], preferred_element_type=F32)
                 + d_ref[:, cb * kin:(cb + 1) * kin] * u2[:, cb * kin:(cb + 1) * kin].astype(F32))
            y_ref[:, cb * kin:(cb + 1) * kin] = y

    y = jax.nn.gelu(y_ref[...], approximate=True)
    gate = jnp.dot(y.astype(BF16), gw_ref[...], preferred_element_type=F32) + gb_ref[...]
    o_ref[...] = (y * jax.nn.sigmoid(gate)).astype(o_ref.dtype).reshape(batch, tc, width)


def _s5(l, proj3, bbar, lam, cre_bd, cim_bd, dskip, glu_w, glu_b):
    batch, seq, _ = proj3.shape
    depth, nb, kin, ns2 = bbar.shape
    ns_blk = ns2 // 2
    width = nb * kin
    npass = lam.shape[1]
    tc = _pick(seq, (S5_TC, 64))
    return pl.pallas_call(
        functools.partial(_s5_body, nb=nb, kin=kin, ns_blk=ns_blk, tc=tc, batch=batch),
        grid_spec=pltpu.PrefetchScalarGridSpec(
            num_scalar_prefetch=1,
            grid=(seq // tc,),
            in_specs=[
                pl.BlockSpec((batch, tc, width), lambda t, l: (0, t, 0)),
                pl.BlockSpec((None, nb, kin, ns2), lambda t, l: (l[0], 0, 0, 0)),
                pl.BlockSpec((None, npass, 2, SUBLANES, ns_blk), lambda t, l: (l[0], 0, 0, 0, 0)),
                pl.BlockSpec((None, nb, ns_blk, kin), lambda t, l: (l[0], 0, 0, 0)),
                pl.BlockSpec((None, nb, ns_blk, kin), lambda t, l: (l[0], 0, 0, 0)),
                pl.BlockSpec((None, 1, width), lambda t, l: (l[0], 0, 0)),
                pl.BlockSpec((None, width, width), lambda t, l: (l[0], 0, 0)),
                pl.BlockSpec((None, 1, width), lambda t, l: (l[0], 0, 0)),
            ],
            out_specs=pl.BlockSpec((batch, tc, width), lambda t, l: (0, t, 0)),
            scratch_shapes=[
                pltpu.VMEM((2 * ns_blk // LANES, SUBLANES * (tc + SUBLANES), LANES), F32),
                pltpu.VMEM((batch * tc, width), F32),
                pltpu.VMEM((npass, 2 * ns_blk // LANES, SUBLANES, LANES), F32),
            ],
        ),
        out_shape=jax.ShapeDtypeStruct((batch, seq, width), ACT),
        compiler_params=_cparams(("arbitrary",)),
        name="s5",
    )(l, proj3, bbar, lam, cre_bd, cim_bd, dskip, glu_w, glu_b)


def _gdn_prepare(hh, head, src_refs, cw_refs, sm_ref, alog_ref, dtb_ref, xpad, qs, ks, vs, gs, bs, *,
                 seq, heads, hd):
    rb = GDN_GROUP
    pad = SUBLANES
    lanes = slice(hh * hd, (hh + 1) * hd)
    for src, cw, dst, mode in zip(src_refs, cw_refs, (qs, ks, vs), "qkv"):
        xpad[pad:pad + seq, :] = src[:, lanes].astype(F32)
        w = cw[:, lanes]
        for r0 in range(0, seq, rb):
            acc = w[0:1, :] * xpad[r0 + pad - 3:r0 + pad - 3 + rb, :]
            for j in range(1, CONV_K):
                acc = acc + w[j:j + 1, :] * xpad[r0 + pad - 3 + j:r0 + pad - 3 + j + rb, :]
            y = _silu(acc)
            if mode != "v":
                y = y * lax.rsqrt(jnp.sum(y * y, axis=-1, keepdims=True) + L2_EPS)
            if mode == "q":
                y = y * (hd ** -0.5)
            dst[r0:r0 + rb, :] = y

    lane = lax.broadcasted_iota(jnp.int32, (rb, LANES), 1)
    neg_a = -jnp.exp(alog_ref[...])
    dtb = dtb_ref[...]
    for r0 in range(0, seq, rb):
        s = sm_ref[r0:r0 + rb, :]
        xa = s + dtb
        sp = jnp.maximum(xa, 0.0) + jnp.log(1.0 + jnp.exp(-jnp.abs(xa)))
        gall = neg_a * sp
        ball = jax.nn.sigmoid(s)
        gcol = jnp.sum(jnp.where(lane == heads + head, gall, 0.0), axis=-1, keepdims=True)
        bcol = jnp.sum(jnp.where(lane == head, ball, 0.0), axis=-1, keepdims=True)
        gs[r0:r0 + rb, :] = jnp.broadcast_to(gcol, (rb, hd))
        bs[r0:r0 + rb, :] = jnp.broadcast_to(bcol, (rb, hd))


def _gdn_solve(hh, r0s, qs, ks, vs, gs, bs, kq_s, n_s, oc_s, gl_s, consts, *, hd):
    rb = GDN_GROUP
    ch = GDN_CHUNK
    causal_ch, strict_ch, cum_m, eye_f, m16, m32_16, m64_32 = consts
    reps = rb // hd
    cpg = rb // ch
    gi = range(len(r0s))

    q = [qs[r0:r0 + rb, :] for r0 in r0s]
    k = [ks[r0:r0 + rb, :] for r0 in r0s]
    v = [vs[r0:r0 + rb, :] for r0 in r0s]
    bb = [bs[r0:r0 + rb, :] for r0 in r0s]
    gcl = [_dot01(cum_m, gs[r0:r0 + rb, :]) for r0 in r0s]
    gc = [t[0:rb, :] for t in gcl]
    gl = [t[rb:2 * rb, :] for t in gcl]
    eg = [jnp.exp(t) for t in gc]
    gcw = [t if reps == 1 else jnp.concatenate([t] * reps, axis=1) for t in gc]
    e = [jnp.exp(jnp.where(causal_ch, t - t.T, 0.0)) for t in gcw]
    kb = [k[i] * bb[i] for i in gi]
    kkqk = [lax.dot_general(jnp.concatenate([kb[i], q[i]], axis=0).astype(BF16), k[i].astype(BF16),
                            (((1,), (1,)), ((), ())), preferred_element_type=F32) for i in gi]
    a = [jnp.where(strict_ch, kkqk[i][0:rb, :] * e[i], 0.0) for i in gi]
    attn = [jnp.where(causal_ch, kkqk[i][rb:2 * rb, :] * e[i], 0.0).astype(BF16) for i in gi]
    p = [-(t * m16) for t in a]
    tinv = [eye_f + t for t in p]
    for _ in range(3):
        p = [_bdot(t, t) for t in p]
        tinv = [tinv[i] + _bdot(tinv[i], p[i]) for i in gi]
    for msk in (m32_16, m64_32):
        low = [_bdot(a[i] * msk, tinv[i]) for i in gi]
        tinv = [tinv[i] - _bdot(tinv[i], low[i]) for i in gi]
    wu = [_bdot(tinv[i], jnp.concatenate([kb[i] * eg[i], v[i] * bb[i]], axis=1)).astype(BF16) for i in gi]
    aw = [jnp.dot(attn[i], wu[i], preferred_element_type=F32) for i in gi]
    qe = [(q[i] * eg[i] - aw[i][:, 0:hd]).astype(BF16) for i in gi]
    kd = [(k[i] * jnp.exp(gl[i] - gc[i])).astype(BF16) for i in gi]
    egl = [jnp.exp(t) for t in gl]
    for i, r0 in enumerate(r0s):
        oc_s[hh, r0:r0 + rb, :] = aw[i][:, hd:2 * hd]
    for c in range(cpg):
        rows = slice(c * ch, (c + 1) * ch)
        kn = [lax.dot_general(kd[i][rows, :], wu[i][rows, :], (((0,), (0,)), ((), ())),
                              preferred_element_type=F32) for i in gi]
        for i, r0 in enumerate(r0s):
            n = r0 // ch + c
            kq_s[hh, n, 0:hd, :] = kn[i][:, 0:hd].astype(BF16)
            kq_s[hh, n, hd:hd + ch, :] = qe[i][rows, :]
            n_s[hh, n] = kn[i][:, hd:2 * hd]
            gl_s[hh, n] = egl[i][c * ch:c * ch + SUBLANES, :]


def _gdn_body(l_ref, q_ref, k_ref, v_ref, z_ref, sm_ref, cwq_ref, cwk_ref, cwv_ref, alog_ref, dtb_ref,
              nw_ref, o_ref, xpad, qs, ks, vs, gs, bs, kq_s, n_s, oc_s, gl_s, *, seq, heads, hd, hp):
    rb = GDN_GROUP
    ch = GDN_CHUNK
    head0 = pl.program_id(1) * hp

    xpad[0:SUBLANES, :] = jnp.zeros((SUBLANES, hd), F32)
    row = lax.broadcasted_iota(jnp.int32, (rb, rb), 0)
    col = lax.broadcasted_iota(jnp.int32, (rb, rb), 1)

    def same(blk):
        return (row // blk) == (col // blk)

    def as_f(m):
        return jnp.where(m, 1.0, 0.0).astype(F32)

    same_ch = same(ch)
    causal_ch = same_ch & (row >= col)
    strict_ch = same_ch & (row > col)
    cum_m = jnp.concatenate([as_f(causal_ch), as_f(same_ch)], axis=0).astype(BF16)
    m16 = as_f(same(16))
    m32 = as_f(same(32))
    consts = (causal_ch, strict_ch, cum_m, as_f(row == col), m16, m32 - m16, as_f(same_ch) - m32)
    step_rows = rb * GDN_INTERLEAVE
    for hh in range(hp):
        _gdn_prepare(hh, head0 + hh, (q_ref, k_ref, v_ref), (cwq_ref, cwk_ref, cwv_ref), sm_ref, alog_ref,
                     dtb_ref, xpad, qs, ks, vs, gs, bs, seq=seq, heads=heads, hd=hd)
        for base in range(0, seq, step_rows):
            r0s = list(range(base, min(base + step_rows, seq), rb))
            _gdn_solve(hh, r0s, qs, ks, vs, gs, bs, kq_s, n_s, oc_s, gl_s, consts, hd=hd)

    nw = nw_ref[...]

    def step(n, states):
        r0 = pl.multiple_of(n * ch, ch)
        new = []
        for hh in range(hp):
            state = states[hh]
            r = jnp.dot(kq_s[hh, n], state.astype(BF16), preferred_element_type=F32)
            out = r[hd:hd + ch, :] + oc_s[hh, pl.ds(r0, ch), :]
            new.append(state * gl_s[hh, n][0:1, :] + (n_s[hh, n] - r[0:hd, :]))
            o = out * lax.rsqrt(jnp.mean(out * out, axis=-1, keepdims=True) + RMS_EPS) * nw
            zz = z_ref[pl.ds(r0, ch), hh * hd:(hh + 1) * hd].astype(F32)
            o_ref[pl.ds(r0, ch), hh * hd:(hh + 1) * hd] = (o * _silu(zz)).astype(o_ref.dtype)
        return tuple(new)

    lax.fori_loop(0, seq // ch, step, tuple(jnp.zeros((hd, hd), F32) for _ in range(hp)))


def _gdn(l, proj, small, conv_w, alog, dtb, norm_w, *, batch, seq, heads, hd, col0):
    m = batch * seq
    hp = GDN_HEADS_PER_STEP
    cb = col0 // hp
    hb = heads // hp
    nch = seq // GDN_CHUNK

    def act_spec(off):
        return pl.BlockSpec((seq, hp * hd), lambda b, h, l, off=off: (b, off + h))

    def cw_spec(off):
        return pl.BlockSpec((None, CONV_K, hp * hd), lambda b, h, l, off=off: (l[0], 0, off + h))

    row_spec = pl.BlockSpec((None, 1, LANES), lambda b, h, l: (l[0], 0, 0))
    return pl.pallas_call(
        functools.partial(_gdn_body, seq=seq, heads=heads, hd=hd, hp=hp),
        grid_spec=pltpu.PrefetchScalarGridSpec(
            num_scalar_prefetch=1,
            grid=(batch, hb),
            in_specs=[
                act_spec(cb), act_spec(cb + hb), act_spec(cb + 2 * hb), act_spec(cb + 3 * hb),
                pl.BlockSpec((seq, LANES), lambda b, h, l: (b, 0)),
                cw_spec(0), cw_spec(hb), cw_spec(2 * hb),
                row_spec, row_spec,
                pl.BlockSpec((None, 1, hd), lambda b, h, l: (l[0], 0, 0)),
            ],
            out_specs=pl.BlockSpec((seq, hp * hd), lambda b, h, l: (b, h)),
            scratch_shapes=(
                [pltpu.VMEM((seq + SUBLANES, hd), F32)]
                + [pltpu.VMEM((seq, hd), F32) for _ in range(5)]
                + [pltpu.VMEM((hp, nch, hd + GDN_CHUNK, hd), BF16),
                   pltpu.VMEM((hp, nch, hd, hd), F32),
                   pltpu.VMEM((hp, seq, hd), F32),
                   pltpu.VMEM((hp, nch, SUBLANES, hd), F32)]
            ),
        ),
        out_shape=jax.ShapeDtypeStruct((m, heads * hd), ACT),
        compiler_params=_cparams(("parallel", "parallel")),
        name="gdn",
    )(l, proj, proj, proj, proj, small, conv_w, conv_w, conv_w, alog, dtb, norm_w)


def _merge_body(l_ref, x_ref, ys_ref, yg_ref, gs0_ref, gs1_ref, gd0_ref, gd1_ref, wbs_ref, wbg_ref, wo_ref,
                g_ref, b_ref, o_ref, *, alpha):
    ys = ys_ref[...].astype(BF16)
    yg = yg_ref[...].astype(BF16)
    hw = wo_ref.shape[0] // 2
    acc = None
    for half, (gsr, gdr) in enumerate(((gs0_ref, gd0_ref), (gs1_ref, gd1_ref))):
        sl = slice(half * hw, (half + 1) * hw)
        mix = (jax.nn.sigmoid(gsr[...].astype(F32)) * jnp.dot(ys, wbs_ref[:, sl], preferred_element_type=F32)
               + jax.nn.sigmoid(gdr[...].astype(F32)) * jnp.dot(yg, wbg_ref[:, sl], preferred_element_type=F32))
        part = jnp.dot(mix.astype(BF16), wo_ref[sl, :], preferred_element_type=F32)
        acc = part if acc is None else acc + part
    y = alpha * x_ref[...] + acc
    o_ref[...] = _layer_norm(y, g_ref[...], b_ref[...])


def _merge(l, x, y_ssm, y_gdn, proj, wbs, wbg, wo, g, b, *, gate_col0, alpha):
    m, d = x.shape
    hw = d // 2
    ws = y_ssm.shape[1]
    wg = y_gdn.shape[1]
    tm = _pick(m, (256, 128))

    def gate_spec(off):
        return pl.BlockSpec((tm, hw), lambda i, l, off=off: (i, gate_col0 + off))

    return pl.pallas_call(
        functools.partial(_merge_body, alpha=alpha),
        grid_spec=pltpu.PrefetchScalarGridSpec(
            num_scalar_prefetch=1,
            grid=(m // tm,),
            in_specs=[
                pl.BlockSpec((tm, d), lambda i, l: (i, 0)),
                pl.BlockSpec((tm, ws), lambda i, l: (i, 0)),
                pl.BlockSpec((tm, wg), lambda i, l: (i, 0)),
                gate_spec(0), gate_spec(1), gate_spec(2), gate_spec(3),
                pl.BlockSpec((None, ws, d), lambda i, l: (l[0], 0, 0)),
                pl.BlockSpec((None, wg, d), lambda i, l: (l[0], 0, 0)),
                pl.BlockSpec((None, d, d), lambda i, l: (l[0], 0, 0)),
                pl.BlockSpec((None, 1, d), lambda i, l: (l[0], 0, 0)),
                pl.BlockSpec((None, 1, d), lambda i, l: (l[0], 0, 0)),
            ],
            out_specs=pl.BlockSpec((tm, d), lambda i, l: (i, 0)),
        ),
        out_shape=jax.ShapeDtypeStruct((m, d), F32),
        compiler_params=_cparams(("parallel",)),
        name="merge",
    )(l, x, y_ssm, y_gdn, proj, proj, proj, proj, wbs, wbg, wo, g, b)


def _block_diag(t, gpb):
    depth, g, r, c = t.shape
    nb = g // gpb
    t5 = t.reshape(depth, nb, gpb, r, c)
    eye = jnp.eye(gpb, dtype=t.dtype)
    out = jnp.einsum("lngrc,gk->lngrkc", t5, eye)
    return out.reshape(depth, nb, gpb * r, gpb * c)


def kernel(x, ffn1_w_gu, ffn1_w_down, ln1_g, ln1_b, w_in, conv_w, ssm_a_re, ssm_a_im, ssm_log_dt, ssm_b_re,
           ssm_b_im, ssm_c_re, ssm_c_im, ssm_d, glu_w, glu_b, gdn_a_log, gdn_dt_bias, gdn_norm_w, w_br_ssm,
           w_br_gdn, w_out, ln2_g, ln2_b, ffn2_w_gu, ffn2_w_down, ln3_g, ln3_b):
    batch, seq, d = x.shape
    depth = ffn1_w_gu.shape[0]
    alpha = (2.0 * depth) ** 0.25
    groups, state, sgrp = ssm_b_re.shape[1:]
    ssm_width = groups * sgrp
    heads = gdn_a_log.shape[1]
    hd = gdn_norm_w.shape[1]
    gdn_width = heads * hd
    n_act = ssm_width + 4 * gdn_width
    n_small = 2 * heads

    def row3(t):
        return t.reshape(depth, 1, -1).astype(F32)

    w_act = w_in[:, :, :n_act].astype(BF16)
    w_gate = w_in[:, :, n_act + n_small:].astype(BF16)
    w_small = jnp.pad(w_in[:, :, n_act:n_act + n_small], ((0, 0), (0, 0), (0, LANES - n_small))).astype(BF16)
    wbs, wbg, wo = w_br_ssm.astype(BF16), w_br_gdn.astype(BF16), w_out.astype(BF16)
    gluw = glu_w.astype(BF16)

    gpb = S5_GROUPS_PER_BLOCK
    bre_bd = _block_diag(jnp.swapaxes(ssm_b_re, 2, 3), gpb)
    bim_bd = _block_diag(jnp.swapaxes(ssm_b_im, 2, 3), gpb)
    cre_bd = _block_diag(jnp.swapaxes(ssm_c_re, 2, 3), gpb)
    cim_bd = _block_diag(jnp.swapaxes(ssm_c_im, 2, 3), gpb)
    nb = groups // gpb

    def blk_rows(t):
        return t.reshape(depth, nb, 1, gpb * state).astype(F32)

    bbar, cre_bd, cim_bd, lam = _s5_prep(blk_rows(ssm_a_re), blk_rows(ssm_a_im),
                                         blk_rows(jnp.repeat(ssm_log_dt, state, axis=1)),
                                         bre_bd, bim_bd, cre_bd, cim_bd, batch=batch)

    alog = jnp.pad(gdn_a_log, ((0, 0), (heads, LANES - 2 * heads))).reshape(depth, 1, LANES)
    dtb = jnp.pad(gdn_dt_bias, ((0, 0), (heads, LANES - 2 * heads))).reshape(depth, 1, LANES)

    def layer(xc, li_):
        l = jnp.reshape(li_, (1,)).astype(jnp.int32)
        xc = _ffn(l, xc, ffn1_w_gu, ffn1_w_down, row3(ln1_g), row3(ln1_b), alpha=alpha)
        proj, small = _inproj(l, xc, w_act, w_gate, w_small)
        y_ssm = _s5(l, proj.reshape(batch, seq, -1), bbar, lam, cre_bd, cim_bd, row3(ssm_d), gluw, row3(glu_b))
        y_ssm = y_ssm.reshape(batch * seq, ssm_width)
        y_gdn = _gdn(l, proj, small, conv_w, alog, dtb, row3(gdn_norm_w), batch=batch, seq=seq, heads=heads,
                     hd=hd, col0=ssm_width // hd)
        xc = _merge(l, xc, y_ssm, y_gdn, proj, wbs, wbg, wo, row3(ln2_g), row3(ln2_b),
                    gate_col0=n_act // (d // 2), alpha=alpha)
        xc = _ffn(l, xc, ffn2_w_gu, ffn2_w_down, row3(ln3_g), row3(ln3_b), alpha=alpha)
        return xc, None

    out, _ = lax.scan(layer, x.reshape(batch * seq, d), jnp.arange(depth, dtype=jnp.int32))
    return out.reshape(batch, seq, d)
```

```python
import functools

import jax
import jax.numpy as jnp
from jax import lax
from jax.experimental import pallas as pl
from jax.experimental.pallas import tpu as pltpu

F32 = jnp.float32
BF16 = jnp.bfloat16
ACT = BF16

LN_EPS = 1e-5
RMS_EPS = 1e-6
L2_EPS = 1e-6
CONV_K = 4
GDN_CHUNK = 64
GDN_GROUP = 128
GDN_INTERLEAVE = 16
GDN_HEADS_PER_STEP = 4
S5_TC = 128
S5_GROUPS_PER_BLOCK = 16
LANES = 128
SUBLANES = 8
VMEM_LIMIT = 56 * 1024 * 1024


def _cparams(sem):
    return pltpu.CompilerParams(dimension_semantics=sem, vmem_limit_bytes=VMEM_LIMIT)


def _pick(n, cands):
    for c in cands:
        if n % c == 0:
            return c
    raise ValueError(f"no tile in {cands} divides {n}")


def _layer_norm(y, g, b):
    mu = jnp.mean(y, axis=-1, keepdims=True)
    yc = y - mu
    var = jnp.mean(yc * yc, axis=-1, keepdims=True)
    return yc * lax.rsqrt(var + LN_EPS) * g + b


def _silu(x):
    return x * jax.nn.sigmoid(x)


def _cmul(ar, ai, br, bi):
    return ar * br - ai * bi, ar * bi + ai * br


def _bdot(a, b):
    return jnp.dot(a.astype(BF16), b.astype(BF16), preferred_element_type=F32)


def _dot01(m01, x):
    x1 = x.astype(BF16)
    r1 = x - x1.astype(F32)
    x2 = r1.astype(BF16)
    x3 = (r1 - x2.astype(F32)).astype(BF16)
    return (jnp.dot(m01, x1, preferred_element_type=F32) + jnp.dot(m01, x2, preferred_element_type=F32)
            + jnp.dot(m01, x3, preferred_element_type=F32))


def _ffn_body(l_ref, x_ref, wg_ref, wu_ref, wd_ref, g_ref, b_ref, o_ref, xb_ref, *, nf, alpha):
    j = pl.program_id(1)

    @pl.when(j == 0)
    def _():
        xb_ref[...] = x_ref[...].astype(BF16)
        o_ref[...] = jnp.zeros_like(o_ref)

    xb = xb_ref[...]
    gate = jnp.dot(xb, wg_ref[...].astype(BF16), preferred_element_type=F32)
    up = jnp.dot(xb, wu_ref[...].astype(BF16), preferred_element_type=F32)
    h = (_silu(gate) * up).astype(BF16)
    o_ref[...] += jnp.dot(h, wd_ref[...].astype(BF16), preferred_element_type=F32)

    @pl.when(j == nf - 1)
    def _():
        y = alpha * x_ref[...] + 0.5 * o_ref[...]
        o_ref[...] = _layer_norm(y, g_ref[...], b_ref[...])


def _ffn(l, x, wgu, wd, g, b, *, alpha):
    m, d = x.shape
    f = wd.shape[1]
    tm = _pick(m, (1024, 512, 256, 128))
    tf = _pick(f, (256, 128))
    nf = f // tf
    return pl.pallas_call(
        functools.partial(_ffn_body, nf=nf, alpha=alpha),
        grid_spec=pltpu.PrefetchScalarGridSpec(
            num_scalar_prefetch=1,
            grid=(m // tm, nf),
            in_specs=[
                pl.BlockSpec((tm, d), lambda i, j, l: (i, 0), pipeline_mode=pl.Buffered(1)),
                pl.BlockSpec((None, d, tf), lambda i, j, l: (l[0], 0, j)),
                pl.BlockSpec((None, d, tf), lambda i, j, l: (l[0], 0, j + nf)),
                pl.BlockSpec((None, tf, d), lambda i, j, l: (l[0], j, 0)),
                pl.BlockSpec((None, 1, d), lambda i, j, l: (l[0], 0, 0)),
                pl.BlockSpec((None, 1, d), lambda i, j, l: (l[0], 0, 0)),
            ],
            out_specs=pl.BlockSpec((tm, d), lambda i, j, l: (i, 0)),
            scratch_shapes=[pltpu.VMEM((tm, d), BF16)],
        ),
        out_shape=jax.ShapeDtypeStruct((m, d), F32),
        compiler_params=_cparams(("parallel", "arbitrary")),
        name="ffn",
    )(l, x, wgu, wgu, wd, g, b)


def _inproj_body(l_ref, x_ref, wa_ref, wg_ref, ws_ref, o_ref, os_ref, xb_ref, *, na):
    j = pl.program_id(1)

    @pl.when(j == 0)
    def _():
        xb_ref[...] = x_ref[...].astype(BF16)
        os_ref[...] = jnp.dot(xb_ref[...], ws_ref[...], preferred_element_type=F32)

    @pl.when(j < na)
    def _():
        o_ref[...] = jnp.dot(xb_ref[...], wa_ref[...], preferred_element_type=F32).astype(o_ref.dtype)

    @pl.when(j >= na)
    def _():
        o_ref[...] = jnp.dot(xb_ref[...], wg_ref[...], preferred_element_type=F32).astype(o_ref.dtype)


def _inproj(l, x, w_act, w_gate, w_small):
    m, d = x.shape
    n_a, n_g = w_act.shape[2], w_gate.shape[2]
    n = n_a + n_g
    tm = _pick(m, (1024, 512, 256, 128))
    tn = next(c for c in (1024, 512, 256, 128) if n_a % c == 0 and n_g % c == 0)
    na = n_a // tn
    return pl.pallas_call(
        functools.partial(_inproj_body, na=na),
        grid_spec=pltpu.PrefetchScalarGridSpec(
            num_scalar_prefetch=1,
            grid=(m // tm, n // tn),
            in_specs=[
                pl.BlockSpec((tm, d), lambda i, j, l: (i, 0)),
                pl.BlockSpec((None, d, tn), lambda i, j, l: (l[0], 0, jnp.minimum(j, na - 1))),
                pl.BlockSpec((None, d, tn), lambda i, j, l: (l[0], 0, jnp.maximum(j - na, 0))),
                pl.BlockSpec((None, d, LANES), lambda i, j, l: (l[0], 0, 0)),
            ],
            out_specs=[
                pl.BlockSpec((tm, tn), lambda i, j, l: (i, j)),
                pl.BlockSpec((tm, LANES), lambda i, j, l: (i, 0)),
            ],
            scratch_shapes=[pltpu.VMEM((tm, d), BF16)],
        ),
        out_shape=[jax.ShapeDtypeStruct((m, n), ACT), jax.ShapeDtypeStruct((m, LANES), F32)],
        compiler_params=_cparams(("parallel", "arbitrary")),
        name="inproj",
    )(l, x, w_act, w_gate, w_small)


def _s5_discretise(lr, li, ldt):
    dt = jnp.exp(ldt)
    mag = jnp.exp(lr * dt)
    lbr = mag * jnp.cos(li * dt)
    lbi = mag * jnp.sin(li * dt)
    den = lr * lr + li * li
    zr, zi = _cmul(lbr - 1.0, lbi, lr / den, -li / den)
    return lbr, lbi, zr, zi


def _block_diag_tile(x, reps, rows_per_blk, cols_per_blk):
    c = cols_per_blk
    kk = lax.broadcasted_iota(jnp.int32, (c, reps * c), 0)
    nn = lax.broadcasted_iota(jnp.int32, (c, reps * c), 1)
    rep = jnp.where((nn & (c - 1)) == kk, 1.0, 0.0).astype(BF16)
    x1 = x.astype(BF16)
    r1 = x - x1.astype(F32)
    x2 = r1.astype(BF16)
    x3 = (r1 - x2.astype(F32)).astype(BF16)
    tiled = (jnp.dot(x1, rep, preferred_element_type=F32) + jnp.dot(x2, rep, preferred_element_type=F32)
             + jnp.dot(x3, rep, preferred_element_type=F32))
    row = lax.broadcasted_iota(jnp.int32, tiled.shape, 0)
    col = lax.broadcasted_iota(jnp.int32, tiled.shape, 1)
    return jnp.where(row // rows_per_blk == col // cols_per_blk, tiled, 0.0)


def _s5_prep_body(lr_ref, li_ref, ldt_ref, lrf_ref, lif_ref, ldtf_ref, bre_ref, bim_ref, cre_ref, cim_ref,
                  bbar_ref, creb_ref, cimb_ref, lam_ref, *, nb, ns_blk, batch, gpb):
    _, _, zr, zi = _s5_discretise(lr_ref[...], li_ref[...], ldt_ref[...])
    sgrp, state = cre_ref.shape[1], bre_ref.shape[1]
    br = _block_diag_tile(bre_ref[...], gpb, sgrp, state)
    bi = _block_diag_tile(bim_ref[...], gpb, sgrp, state)
    bbar_ref[:, 0:ns_blk] = (zr * br - zi * bi).astype(BF16)
    bbar_ref[:, ns_blk:2 * ns_blk] = (zr * bi + zi * br).astype(BF16)
    creb_ref[...] = _block_diag_tile(cre_ref[...], gpb, state, sgrp).astype(BF16)
    cimb_ref[...] = _block_diag_tile(cim_ref[...], gpb, state, sgrp).astype(BF16)

    @pl.when(pl.program_id(1) == 0)
    def _():
        spb = SUBLANES // batch
        row = lax.broadcasted_iota(jnp.int32, (SUBLANES, ns_blk), 0)
        for ps in range(nb // spb):
            tiles = [jnp.zeros((SUBLANES, ns_blk), F32), jnp.zeros((SUBLANES, ns_blk), F32)]
            for ci in range(spb):
                cb = ps * spb + ci
                lb = _s5_discretise(lrf_ref[cb], lif_ref[cb], ldtf_ref[cb])[0:2]
                for part in range(2):
                    blk = jnp.broadcast_to(lb[part], (SUBLANES, ns_blk))
                    tiles[part] = jnp.where(row // batch == ci, blk, tiles[part])
            lam_ref[ps, 0] = tiles[0]
            lam_ref[ps, 1] = tiles[1]


def _s5_prep(lr, li, ldt, bre_t, bim_t, cre_t, cim_t, *, batch, gpb):
    depth, nb, kin, state = bre_t.shape
    ns_blk = gpb * state
    assert SUBLANES % batch == 0 and nb % (SUBLANES // batch) == 0
    assert state & (state - 1) == 0 and (kin // gpb) & (kin // gpb - 1) == 0
    npass = nb // (SUBLANES // batch)
    blk_row = pl.BlockSpec((None, None, 1, ns_blk), lambda l, c: (l, c, 0, 0))
    all_rows = pl.BlockSpec((None, nb, 1, ns_blk), lambda l, c: (l, 0, 0, 0))
    b_in = pl.BlockSpec((None, None, kin, state), lambda l, c: (l, c, 0, 0))
    c_in = pl.BlockSpec((None, None, ns_blk, kin // gpb), lambda l, c: (l, c, 0, 0))
    c_out = pl.BlockSpec((None, None, ns_blk, kin), lambda l, c: (l, c, 0, 0))
    return pl.pallas_call(
        functools.partial(_s5_prep_body, nb=nb, ns_blk=ns_blk, batch=batch, gpb=gpb),
        grid=(depth, nb),
        in_specs=[blk_row, blk_row, blk_row, all_rows, all_rows, all_rows, b_in, b_in, c_in, c_in],
        out_specs=[
            pl.BlockSpec((None, None, kin, 2 * ns_blk), lambda l, c: (l, c, 0, 0)),
            c_out, c_out,
            pl.BlockSpec((None, npass, 2, SUBLANES, ns_blk), lambda l, c: (l, 0, 0, 0, 0)),
        ],
        out_shape=[
            jax.ShapeDtypeStruct((depth, nb, kin, 2 * ns_blk), BF16),
            jax.ShapeDtypeStruct((depth, nb, ns_blk, kin), BF16),
            jax.ShapeDtypeStruct((depth, nb, ns_blk, kin), BF16),
            jax.ShapeDtypeStruct((depth, npass, 2, SUBLANES, ns_blk), F32),
        ],
        compiler_params=_cparams(("arbitrary", "arbitrary")),
        name="s5_prep",
    )(lr, li, ldt, lr, li, ldt, bre_t, bim_t, cre_t, cim_t)


def _s5_body(l_ref, u_ref, bbar_ref, lam_ref, cre_ref, cim_ref, d_ref, gw_ref, gb_ref, o_ref,
             buf, y_ref, carry_ref, *, nb, kin, ns_blk, tc, batch):
    t = pl.program_id(0)
    width = nb * kin
    pitch = tc + SUBLANES
    nsl = ns_blk // LANES
    spb = SUBLANES // batch

    @pl.when(t == 0)
    def _():
        carry_ref[...] = jnp.zeros_like(carry_ref)

    u2 = u_ref[...].reshape(batch * tc, width)
    for ps in range(nb // spb):
        for ci in range(spb):
            cb = ps * spb + ci
            bu = jnp.dot(u2[:, cb * kin:(cb + 1) * kin].astype(BF16), bbar_ref[cb], preferred_element_type=F32)
            for b in range(batch):
                r0 = (ci * batch + b) * pitch
                for j in range(2 * nsl):
                    buf[j, r0:r0 + tc, :] = bu[b * tc:(b + 1) * tc, j * LANES:(j + 1) * LANES]

        def scan_step(i, carry, ps=ps):
            new = []
            for j in range(nsl):
                ar = lam_ref[ps, 0, :, j * LANES:(j + 1) * LANES]
                ai = lam_ref[ps, 1, :, j * LANES:(j + 1) * LANES]
                sr, si = carry[j], carry[nsl + j]
                xr = buf[j, pl.ds(i, SUBLANES, stride=pitch), :]
                xi = buf[nsl + j, pl.ds(i, SUBLANES, stride=pitch), :]
                nr = ar * sr - ai * si + xr
                ni = ar * si + ai * sr + xi
                buf[j, pl.ds(i, SUBLANES, stride=pitch), :] = nr
                buf[nsl + j, pl.ds(i, SUBLANES, stride=pitch), :] = ni
                new.append((nr, ni))
            return tuple(x[0] for x in new) + tuple(x[1] for x in new)

        init = tuple(carry_ref[ps, j] for j in range(2 * nsl))
        fin = lax.fori_loop(0, tc, scan_step, init, unroll=8)
        for j in range(2 * nsl):
            carry_ref[ps, j] = fin[j]

        for ci in range(spb):
            cb = ps * spb + ci

            def gather(part, ci=ci):
                return jnp.concatenate(
                    [jnp.concatenate([buf[part * nsl + j, (ci * batch + b) * pitch:(ci * batch + b) * pitch + tc, :]
                                      for j in range(nsl)], axis=1) for b in range(batch)], axis=0).astype(BF16)

            y = (jnp.dot(gather(0), cre_ref[cb], preferred_element_type=F32)
                 - jnp.dot(gather(1), cim_ref[cb], preferred_element_type=F32)
                 + d_ref[:, cb * kin:(cb + 1) * kin] * u2[:, cb * kin:(cb + 1) * kin].astype(F32))
            y_ref[:, cb * kin:(cb + 1) * kin] = y

    y = jax.nn.gelu(y_ref[...], approximate=True)
    gate = jnp.dot(y.astype(BF16), gw_ref[...], preferred_element_type=F32) + gb_ref[...]
    o_ref[...] = (y * jax.nn.sigmoid(gate)).astype(o_ref.dtype).reshape(batch, tc, width)


def _s5(l, proj3, bbar, lam, cre_bd, cim_bd, dskip, glu_w, glu_b):
    batch, seq, _ = proj3.shape
    depth, nb, kin, ns2 = bbar.shape
    ns_blk = ns2 // 2
    width = nb * kin
    npass = lam.shape[1]
    tc = _pick(seq, (S5_TC, 64))
    return pl.pallas_call(
        functools.partial(_s5_body, nb=nb, kin=kin, ns_blk=ns_blk, tc=tc, batch=batch),
        grid_spec=pltpu.PrefetchScalarGridSpec(
            num_scalar_prefetch=1,
            grid=(seq // tc,),
            in_specs=[
                pl.BlockSpec((batch, tc, width), lambda t, l: (0, t, 0)),
                pl.BlockSpec((None, nb, kin, ns2), lambda t, l: (l[0], 0, 0, 0)),
                pl.BlockSpec((None, npass, 2, SUBLANES, ns_blk), lambda t, l: (l[0], 0, 0, 0, 0)),
                pl.BlockSpec((None, nb, ns_blk, kin), lambda t, l: (l[0], 0, 0, 0)),
                pl.BlockSpec((None, nb, ns_blk, kin), lambda t, l: (l[0], 0, 0, 0)),
                pl.BlockSpec((None, 1, width), lambda t, l: (l[0], 0, 0)),
                pl.BlockSpec((None, width, width), lambda t, l: (l[0], 0, 0)),
                pl.BlockSpec((None, 1, width), lambda t, l: (l[0], 0, 0)),
            ],
            out_specs=pl.BlockSpec((batch, tc, width), lambda t, l: (0, t, 0)),
            scratch_shapes=[
                pltpu.VMEM((2 * ns_blk // LANES, SUBLANES * (tc + SUBLANES), LANES), F32),
                pltpu.VMEM((batch * tc, width), F32),
                pltpu.VMEM((npass, 2 * ns_blk // LANES, SUBLANES, LANES), F32),
            ],
        ),
        out_shape=jax.ShapeDtypeStruct((batch, seq, width), ACT),
        compiler_params=_cparams(("arbitrary",)),
        name="s5",
    )(l, proj3, bbar, lam, cre_bd, cim_bd, dskip, glu_w, glu_b)


def _gdn_prepare(hh, head, src_refs, cw_refs, bg_s, xpad, qs, ks, vs, gs, bs, *, seq, heads, hd):
    rb = GDN_GROUP
    pad = SUBLANES
    lanes = slice(hh * hd, (hh + 1) * hd)
    for src, cw, dst, mode in zip(src_refs, cw_refs, (qs, ks, vs), "qkv"):
        xpad[pad:pad + seq, :] = src[:, lanes].astype(F32)
        w = cw[:, lanes]
        for r0 in range(0, seq, rb):
            acc = w[0:1, :] * xpad[r0 + pad - 3:r0 + pad - 3 + rb, :]
            for j in range(1, CONV_K):
                acc = acc + w[j:j + 1, :] * xpad[r0 + pad - 3 + j:r0 + pad - 3 + j + rb, :]
            y = _silu(acc)
            if mode != "v":
                y = y * lax.rsqrt(jnp.sum(y * y, axis=-1, keepdims=True) + L2_EPS)
            if mode == "q":
                y = y * (hd ** -0.5)
            dst[r0:r0 + rb, :] = y

    lane = lax.broadcasted_iota(jnp.int32, (rb, LANES), 1)
    for r0 in range(0, seq, rb):
        bg = bg_s[r0:r0 + rb, :]
        gcol = jnp.sum(jnp.where(lane == heads + head, bg, 0.0), axis=-1, keepdims=True)
        bcol = jnp.sum(jnp.where(lane == head, bg, 0.0), axis=-1, keepdims=True)
        gs[r0:r0 + rb, :] = jnp.broadcast_to(gcol, (rb, hd))
        bs[r0:r0 + rb, :] = jnp.broadcast_to(bcol, (rb, hd))


def _gdn_gate_terms(sm_ref, alog_ref, dtb_ref, bg_s, *, seq, heads):
    rb = GDN_GROUP
    lane = lax.broadcasted_iota(jnp.int32, (rb, LANES), 1)
    neg_a = -jnp.exp(alog_ref[...])
    dtb = dtb_ref[...]
    for r0 in range(0, seq, rb):
        s = sm_ref[r0:r0 + rb, :]
        xa = s + dtb
        sp = jnp.maximum(xa, 0.0) + jnp.log(1.0 + jnp.exp(-jnp.abs(xa)))
        bg_s[r0:r0 + rb, :] = jnp.where(lane < heads, jax.nn.sigmoid(s), neg_a * sp)


def _gdn_solve(hh, r0s, qs, ks, vs, gs, bs, kq_s, n_s, oc_s, gl_s, consts, *, hd):
    rb = GDN_GROUP
    ch = GDN_CHUNK
    causal_ch, strict_ch, cum_m, eye_f, m16, m32_16, m64_32 = consts
    reps = rb // hd
    cpg = rb // ch
    gi = range(len(r0s))

    q = [qs[r0:r0 + rb, :] for r0 in r0s]
    k = [ks[r0:r0 + rb, :] for r0 in r0s]
    v = [vs[r0:r0 + rb, :] for r0 in r0s]
    bb = [bs[r0:r0 + rb, :] for r0 in r0s]
    gcl = [_dot01(cum_m, gs[r0:r0 + rb, :]) for r0 in r0s]
    gc = [t[0:rb, :] for t in gcl]
    gl = [t[rb:2 * rb, :] for t in gcl]
    eg = [jnp.exp(t) for t in gc]
    gcw = [t if reps == 1 else jnp.concatenate([t] * reps, axis=1) for t in gc]
    e = [jnp.exp(jnp.where(causal_ch, t - t.T, 0.0)) for t in gcw]
    kb = [k[i] * bb[i] for i in gi]
    kkqk = [lax.dot_general(jnp.concatenate([kb[i], q[i]], axis=0).astype(BF16), k[i].astype(BF16),
                            (((1,), (1,)), ((), ())), preferred_element_type=F32) for i in gi]
    a = [jnp.where(strict_ch, kkqk[i][0:rb, :] * e[i], 0.0) for i in gi]
    attn = [jnp.where(causal_ch, kkqk[i][rb:2 * rb, :] * e[i], 0.0).astype(BF16) for i in gi]
    p = [-(t * m16) for t in a]
    tinv = [eye_f + t for t in p]
    for _ in range(3):
        p = [_bdot(t, t) for t in p]
        tinv = [tinv[i] + _bdot(tinv[i], p[i]) for i in gi]
    for msk in (m32_16, m64_32):
        low = [_bdot(a[i] * msk, tinv[i]) for i in gi]
        tinv = [tinv[i] - _bdot(tinv[i], low[i]) for i in gi]
    wu = [_bdot(tinv[i], jnp.concatenate([kb[i] * eg[i], v[i] * bb[i]], axis=1)).astype(BF16) for i in gi]
    aw = [jnp.dot(attn[i], wu[i], preferred_element_type=F32) for i in gi]
    qe = [(q[i] * eg[i] - aw[i][:, 0:hd]).astype(BF16) for i in gi]
    kd = [(k[i] * jnp.exp(gl[i] - gc[i])).astype(BF16) for i in gi]
    egl = [jnp.exp(t) for t in gl]
    for i, r0 in enumerate(r0s):
        oc_s[hh, r0:r0 + rb, :] = aw[i][:, hd:2 * hd]
    for c in range(cpg):
        rows = slice(c * ch, (c + 1) * ch)
        kn = [lax.dot_general(kd[i][rows, :], wu[i][rows, :], (((0,), (0,)), ((), ())),
                              preferred_element_type=F32) for i in gi]
        for i, r0 in enumerate(r0s):
            n = r0 // ch + c
            kq_s[hh, n, 0:hd, :] = kn[i][:, 0:hd].astype(BF16)
            kq_s[hh, n, hd:hd + ch, :] = qe[i][rows, :]
            n_s[hh, n] = kn[i][:, hd:2 * hd]
            gl_s[hh, n] = egl[i][c * ch:c * ch + SUBLANES, :]


def _gdn_body(l_ref, q_ref, k_ref, v_ref, z_ref, sm_ref, cwq_ref, cwk_ref, cwv_ref, alog_ref, dtb_ref,
              nw_ref, o_ref, xpad, bg_s, qs, ks, vs, gs, bs, kq_s, n_s, oc_s, gl_s, *, seq, heads, hd, hp):
    rb = GDN_GROUP
    ch = GDN_CHUNK
    head0 = pl.program_id(1) * hp

    xpad[0:SUBLANES, :] = jnp.zeros((SUBLANES, hd), F32)
    _gdn_gate_terms(sm_ref, alog_ref, dtb_ref, bg_s, seq=seq, heads=heads)
    row = lax.broadcasted_iota(jnp.int32, (rb, rb), 0)
    col = lax.broadcasted_iota(jnp.int32, (rb, rb), 1)

    def same(blk):
        return (row // blk) == (col // blk)

    def as_f(m):
        return jnp.where(m, 1.0, 0.0).astype(F32)

    same_ch = same(ch)
    causal_ch = same_ch & (row >= col)
    strict_ch = same_ch & (row > col)
    cum_m = jnp.concatenate([as_f(causal_ch), as_f(same_ch)], axis=0).astype(BF16)
    m16 = as_f(same(16))
    m32 = as_f(same(32))
    consts = (causal_ch, strict_ch, cum_m, as_f(row == col), m16, m32 - m16, as_f(same_ch) - m32)
    step_rows = rb * GDN_INTERLEAVE
    for hh in range(hp):
        _gdn_prepare(hh, head0 + hh, (q_ref, k_ref, v_ref), (cwq_ref, cwk_ref, cwv_ref), bg_s, xpad, qs, ks, vs,
                     gs, bs, seq=seq, heads=heads, hd=hd)
        for base in range(0, seq, step_rows):
            r0s = list(range(base, min(base + step_rows, seq), rb))
            _gdn_solve(hh, r0s, qs, ks, vs, gs, bs, kq_s, n_s, oc_s, gl_s, consts, hd=hd)

    nw = nw_ref[...]

    def step(n, states):
        r0 = pl.multiple_of(n * ch, ch)
        new = []
        for hh in range(hp):
            state = states[hh]
            r = jnp.dot(kq_s[hh, n], state.astype(BF16), preferred_element_type=F32)
            out = r[hd:hd + ch, :] + oc_s[hh, pl.ds(r0, ch), :]
            new.append(state * gl_s[hh, n][0:1, :] + (n_s[hh, n] - r[0:hd, :]))
            o = out * lax.rsqrt(jnp.mean(out * out, axis=-1, keepdims=True) + RMS_EPS) * nw
            zz = z_ref[pl.ds(r0, ch), hh * hd:(hh + 1) * hd].astype(F32)
            o_ref[pl.ds(r0, ch), hh * hd:(hh + 1) * hd] = (o * _silu(zz)).astype(o_ref.dtype)
        return tuple(new)

    lax.fori_loop(0, seq // ch, step, tuple(jnp.zeros((hd, hd), F32) for _ in range(hp)), unroll=2)


def _gdn(l, proj, small, conv_w, alog, dtb, norm_w, *, batch, seq, heads, hd, col0):
    m = batch * seq
    hp = GDN_HEADS_PER_STEP
    cb = col0 // hp
    hb = heads // hp
    nch = seq // GDN_CHUNK

    def act_spec(off):
        return pl.BlockSpec((seq, hp * hd), lambda b, h, l, off=off: (b, off + h))

    def cw_spec(off):
        return pl.BlockSpec((None, CONV_K, hp * hd), lambda b, h, l, off=off: (l[0], 0, off + h))

    row_spec = pl.BlockSpec((None, 1, LANES), lambda b, h, l: (l[0], 0, 0))
    return pl.pallas_call(
        functools.partial(_gdn_body, seq=seq, heads=heads, hd=hd, hp=hp),
        grid_spec=pltpu.PrefetchScalarGridSpec(
            num_scalar_prefetch=1,
            grid=(batch, hb),
            in_specs=[
                act_spec(cb), act_spec(cb + hb), act_spec(cb + 2 * hb), act_spec(cb + 3 * hb),
                pl.BlockSpec((seq, LANES), lambda b, h, l: (b, 0), pipeline_mode=pl.Buffered(1)),
                cw_spec(0), cw_spec(hb), cw_spec(2 * hb),
                row_spec, row_spec,
                pl.BlockSpec((None, 1, hd), lambda b, h, l: (l[0], 0, 0)),
            ],
            out_specs=pl.BlockSpec((seq, hp * hd), lambda b, h, l: (b, h)),
            scratch_shapes=(
                [pltpu.VMEM((seq + SUBLANES, hd), F32), pltpu.VMEM((seq, LANES), F32)]
                + [pltpu.VMEM((seq, hd), F32) for _ in range(5)]
                + [pltpu.VMEM((hp, nch, hd + GDN_CHUNK, hd), BF16),
                   pltpu.VMEM((hp, nch, hd, hd), F32),
                   pltpu.VMEM((hp, seq, hd), F32),
                   pltpu.VMEM((hp, nch, SUBLANES, hd), F32)]
            ),
        ),
        out_shape=jax.ShapeDtypeStruct((m, heads * hd), ACT),
        compiler_params=_cparams(("parallel", "parallel")),
        name="gdn",
    )(l, proj, proj, proj, proj, small, conv_w, conv_w, conv_w, alog, dtb, norm_w)


def _merge_body(l_ref, x_ref, ys_ref, yg_ref, gs0_ref, gs1_ref, gd0_ref, gd1_ref, wbs_ref, wbg_ref, wo_ref,
                g_ref, b_ref, o_ref, *, alpha):
    ys = ys_ref[...].astype(BF16)
    yg = yg_ref[...].astype(BF16)
    hw = wo_ref.shape[0] // 2
    acc = None
    for half, (gsr, gdr) in enumerate(((gs0_ref, gd0_ref), (gs1_ref, gd1_ref))):
        sl = slice(half * hw, (half + 1) * hw)
        mix = (jax.nn.sigmoid(gsr[...].astype(F32)) * jnp.dot(ys, wbs_ref[:, sl], preferred_element_type=F32)
               + jax.nn.sigmoid(gdr[...].astype(F32)) * jnp.dot(yg, wbg_ref[:, sl], preferred_element_type=F32))
        part = jnp.dot(mix.astype(BF16), wo_ref[sl, :], preferred_element_type=F32)
        acc = part if acc is None else acc + part
    y = alpha * x_ref[...] + acc
    o_ref[...] = _layer_norm(y, g_ref[...], b_ref[...])


def _merge(l, x, y_ssm, y_gdn, proj, wbs, wbg, wo, g, b, *, gate_col0, alpha):
    m, d = x.shape
    hw = d // 2
    ws = y_ssm.shape[1]
    wg = y_gdn.shape[1]
    tm = _pick(m, (512, 256, 128))

    def gate_spec(off):
        return pl.BlockSpec((tm, hw), lambda i, l, off=off: (i, gate_col0 + off))

    return pl.pallas_call(
        functools.partial(_merge_body, alpha=alpha),
        grid_spec=pltpu.PrefetchScalarGridSpec(
            num_scalar_prefetch=1,
            grid=(m // tm,),
            in_specs=[
                pl.BlockSpec((tm, d), lambda i, l: (i, 0)),
                pl.BlockSpec((tm, ws), lambda i, l: (i, 0)),
                pl.BlockSpec((tm, wg), lambda i, l: (i, 0)),
                gate_spec(0), gate_spec(1), gate_spec(2), gate_spec(3),
                pl.BlockSpec((None, ws, d), lambda i, l: (l[0], 0, 0), pipeline_mode=pl.Buffered(1)),
                pl.BlockSpec((None, wg, d), lambda i, l: (l[0], 0, 0), pipeline_mode=pl.Buffered(1)),
                pl.BlockSpec((None, d, d), lambda i, l: (l[0], 0, 0), pipeline_mode=pl.Buffered(1)),
                pl.BlockSpec((None, 1, d), lambda i, l: (l[0], 0, 0)),
                pl.BlockSpec((None, 1, d), lambda i, l: (l[0], 0, 0)),
            ],
            out_specs=pl.BlockSpec((tm, d), lambda i, l: (i, 0)),
        ),
        out_shape=jax.ShapeDtypeStruct((m, d), F32),
        compiler_params=_cparams(("parallel",)),
        name="merge",
    )(l, x, y_ssm, y_gdn, proj, proj, proj, proj, wbs, wbg, wo, g, b)


def kernel(x, ffn1_w_gu, ffn1_w_down, ln1_g, ln1_b, w_in, conv_w, ssm_a_re, ssm_a_im, ssm_log_dt, ssm_b_re,
           ssm_b_im, ssm_c_re, ssm_c_im, ssm_d, glu_w, glu_b, gdn_a_log, gdn_dt_bias, gdn_norm_w, w_br_ssm,
           w_br_gdn, w_out, ln2_g, ln2_b, ffn2_w_gu, ffn2_w_down, ln3_g, ln3_b):
    batch, seq, d = x.shape
    depth = ffn1_w_gu.shape[0]
    alpha = (2.0 * depth) ** 0.25
    groups, state, sgrp = ssm_b_re.shape[1:]
    ssm_width = groups * sgrp
    heads = gdn_a_log.shape[1]
    hd = gdn_norm_w.shape[1]
    gdn_width = heads * hd
    n_act = ssm_width + 4 * gdn_width
    n_small = 2 * heads

    def row3(t):
        return t.reshape(depth, 1, -1).astype(F32)

    w_act = w_in[:, :, :n_act].astype(BF16)
    w_gate = w_in[:, :, n_act + n_small:].astype(BF16)
    w_small = jnp.pad(w_in[:, :, n_act:n_act + n_small], ((0, 0), (0, 0), (0, LANES - n_small))).astype(BF16)
    wbs, wbg, wo = w_br_ssm.astype(BF16), w_br_gdn.astype(BF16), w_out.astype(BF16)
    gluw = glu_w.astype(BF16)

    gpb = S5_GROUPS_PER_BLOCK
    nb = groups // gpb

    def b_stack(t):
        return jnp.swapaxes(t, 2, 3).reshape(depth, nb, gpb * sgrp, state).astype(F32)

    def c_stack(t):
        return jnp.swapaxes(t, 2, 3).reshape(depth, nb, gpb * state, sgrp).astype(F32)

    def blk_rows(t):
        return t.reshape(depth, nb, 1, gpb * state).astype(F32)

    bbar, cre_bd, cim_bd, lam = _s5_prep(blk_rows(ssm_a_re), blk_rows(ssm_a_im),
                                         blk_rows(jnp.repeat(ssm_log_dt, state, axis=1)),
                                         b_stack(ssm_b_re), b_stack(ssm_b_im), c_stack(ssm_c_re),
                                         c_stack(ssm_c_im), batch=batch, gpb=gpb)

    alog = jnp.pad(gdn_a_log, ((0, 0), (heads, LANES - 2 * heads))).reshape(depth, 1, LANES)
    dtb = jnp.pad(gdn_dt_bias, ((0, 0), (heads, LANES - 2 * heads))).reshape(depth, 1, LANES)

    def layer(xc, li_):
        l = jnp.reshape(li_, (1,)).astype(jnp.int32)
        xc = _ffn(l, xc, ffn1_w_gu, ffn1_w_down, row3(ln1_g), row3(ln1_b), alpha=alpha)
        proj, small = _inproj(l, xc, w_act, w_gate, w_small)
        y_ssm = _s5(l, proj.reshape(batch, seq, -1), bbar, lam, cre_bd, cim_bd, row3(ssm_d), gluw, row3(glu_b))
        y_ssm = y_ssm.reshape(batch * seq, ssm_width)
        y_gdn = _gdn(l, proj, small, conv_w, alog, dtb, row3(gdn_norm_w), batch=batch, seq=seq, heads=heads,
                     hd=hd, col0=ssm_width // hd)
        xc = _merge(l, xc, y_ssm, y_gdn, proj, wbs, wbg, wo, row3(ln2_g), row3(ln2_b),
                    gate_col0=n_act // (d // 2), alpha=alpha)
        xc = _ffn(l, xc, ffn2_w_gu, ffn2_w_down, row3(ln3_g), row3(ln3_b), alpha=alpha)
        return xc, None

    out, _ = lax.scan(layer, x.reshape(batch * seq, d), jnp.arange(depth, dtype=jnp.int32))
    return out.reshape(batch, seq, d)
```

```python
import functools

import jax
import jax.numpy as jnp
from jax import lax
from jax.experimental import pallas as pl
from jax.experimental.pallas import tpu as pltpu

F32 = jnp.float32
BF16 = jnp.bfloat16
ACT = BF16

LN_EPS = 1e-5
RMS_EPS = 1e-6
L2_EPS = 1e-6
CONV_K = 4
GDN_CHUNK = 64
GDN_GROUP = 128
GDN_INTERLEAVE = 16
GDN_HEADS_PER_STEP = 4
S5_TC = 128
S5_GROUPS_PER_BLOCK = 16
LANES = 128
SUBLANES = 8
VMEM_LIMIT = 56 * 1024 * 1024


def _cparams(sem):
    return pltpu.CompilerParams(dimension_semantics=sem, vmem_limit_bytes=VMEM_LIMIT)


def _pick(n, cands):
    for c in cands:
        if n % c == 0:
            return c
    raise ValueError(f"no tile in {cands} divides {n}")


def _layer_norm(y, g, b):
    mu = jnp.mean(y, axis=-1, keepdims=True)
    yc = y - mu
    var = jnp.mean(yc * yc, axis=-1, keepdims=True)
    return yc * lax.rsqrt(var + LN_EPS) * g + b


def _silu(x):
    return x * jax.nn.sigmoid(x)


def _cmul(ar, ai, br, bi):
    return ar * br - ai * bi, ar * bi + ai * br


def _bdot(a, b):
    return jnp.dot(a.astype(BF16), b.astype(BF16), preferred_element_type=F32)


def _dot01(m01, x):
    x1 = x.astype(BF16)
    r1 = x - x1.astype(F32)
    x2 = r1.astype(BF16)
    x3 = (r1 - x2.astype(F32)).astype(BF16)
    return (jnp.dot(m01, x1, preferred_element_type=F32) + jnp.dot(m01, x2, preferred_element_type=F32)
            + jnp.dot(m01, x3, preferred_element_type=F32))


def _ffn_body(l_ref, x_ref, wg_ref, wu_ref, wd_ref, g_ref, b_ref, o_ref, xb_ref, *, nf, alpha):
    j = pl.program_id(1)

    @pl.when(j == 0)
    def _():
        xb_ref[...] = x_ref[...].astype(BF16)
        o_ref[...] = jnp.zeros_like(o_ref)

    xb = xb_ref[...]
    gate = jnp.dot(xb, wg_ref[...].astype(BF16), preferred_element_type=F32)
    up = jnp.dot(xb, wu_ref[...].astype(BF16), preferred_element_type=F32)
    h = (_silu(gate) * up).astype(BF16)
    o_ref[...] += jnp.dot(h, wd_ref[...].astype(BF16), preferred_element_type=F32)

    @pl.when(j == nf - 1)
    def _():
        y = alpha * x_ref[...] + 0.5 * o_ref[...]
        o_ref[...] = _layer_norm(y, g_ref[...], b_ref[...])


def _ffn(l, x, wgu, wd, g, b, *, alpha):
    m, d = x.shape
    f = wd.shape[1]
    tm = _pick(m, (1024, 512, 256, 128))
    tf = _pick(f, (256, 128))
    nf = f // tf
    return pl.pallas_call(
        functools.partial(_ffn_body, nf=nf, alpha=alpha),
        grid_spec=pltpu.PrefetchScalarGridSpec(
            num_scalar_prefetch=1,
            grid=(m // tm, nf),
            in_specs=[
                pl.BlockSpec((tm, d), lambda i, j, l: (i, 0), pipeline_mode=pl.Buffered(1)),
                pl.BlockSpec((None, d, tf), lambda i, j, l: (l[0], 0, j)),
                pl.BlockSpec((None, d, tf), lambda i, j, l: (l[0], 0, j + nf)),
                pl.BlockSpec((None, tf, d), lambda i, j, l: (l[0], j, 0)),
                pl.BlockSpec((None, 1, d), lambda i, j, l: (l[0], 0, 0)),
                pl.BlockSpec((None, 1, d), lambda i, j, l: (l[0], 0, 0)),
            ],
            out_specs=pl.BlockSpec((tm, d), lambda i, j, l: (i, 0)),
            scratch_shapes=[pltpu.VMEM((tm, d), BF16)],
        ),
        out_shape=jax.ShapeDtypeStruct((m, d), F32),
        compiler_params=_cparams(("parallel", "arbitrary")),
        name="ffn",
    )(l, x, wgu, wgu, wd, g, b)


def _inproj_body(l_ref, x_ref, w_ref, ws_ref, wt_ref, o_ref, os_ref, ot_ref, xb_ref):
    j = pl.program_id(1)

    @pl.when(j == 0)
    def _():
        xb_ref[...] = x_ref[...].astype(BF16)
        os_ref[...] = jnp.dot(xb_ref[...], ws_ref[...], preferred_element_type=F32)
        ot_ref[...] = jnp.dot(xb_ref[...], wt_ref[...], preferred_element_type=F32).astype(ot_ref.dtype)

    o_ref[...] = jnp.dot(xb_ref[...], w_ref[...].astype(BF16), preferred_element_type=F32).astype(o_ref.dtype)


def _inproj(l, x, w_in, w_small, w_tail, *, n_full):
    m, d = x.shape
    tm = _pick(m, (1024, 512, 256, 128))
    tn = _pick(n_full, (1024, 512, 256, 128))
    lane_blk = pl.BlockSpec((None, d, LANES), lambda i, j, l: (l[0], 0, 0))
    row_blk = pl.BlockSpec((tm, LANES), lambda i, j, l: (i, 0))
    return pl.pallas_call(
        _inproj_body,
        grid_spec=pltpu.PrefetchScalarGridSpec(
            num_scalar_prefetch=1,
            grid=(m // tm, n_full // tn),
            in_specs=[
                pl.BlockSpec((tm, d), lambda i, j, l: (i, 0)),
                pl.BlockSpec((None, d, tn), lambda i, j, l: (l[0], 0, j)),
                lane_blk, lane_blk,
            ],
            out_specs=[pl.BlockSpec((tm, tn), lambda i, j, l: (i, j)), row_blk, row_blk],
            scratch_shapes=[pltpu.VMEM((tm, d), BF16)],
        ),
        out_shape=[jax.ShapeDtypeStruct((m, n_full), ACT), jax.ShapeDtypeStruct((m, LANES), F32),
                   jax.ShapeDtypeStruct((m, LANES), ACT)],
        compiler_params=_cparams(("parallel", "arbitrary")),
        name="inproj",
    )(l, x, w_in, w_small, w_tail)


def _s5_discretise(lr, li, ldt):
    dt = jnp.exp(ldt)
    mag = jnp.exp(lr * dt)
    lbr = mag * jnp.cos(li * dt)
    lbi = mag * jnp.sin(li * dt)
    den = lr * lr + li * li
    zr, zi = _cmul(lbr - 1.0, lbi, lr / den, -li / den)
    return lbr, lbi, zr, zi


def _block_diag_tile(x, reps, rows_per_blk, cols_per_blk):
    c = cols_per_blk
    kk = lax.broadcasted_iota(jnp.int32, (c, reps * c), 0)
    nn = lax.broadcasted_iota(jnp.int32, (c, reps * c), 1)
    rep = jnp.where((nn & (c - 1)) == kk, 1.0, 0.0).astype(BF16)
    x1 = x.astype(BF16)
    r1 = x - x1.astype(F32)
    x2 = r1.astype(BF16)
    x3 = (r1 - x2.astype(F32)).astype(BF16)
    tiled = (jnp.dot(x1, rep, preferred_element_type=F32) + jnp.dot(x2, rep, preferred_element_type=F32)
             + jnp.dot(x3, rep, preferred_element_type=F32))
    row = lax.broadcasted_iota(jnp.int32, tiled.shape, 0)
    col = lax.broadcasted_iota(jnp.int32, tiled.shape, 1)
    return jnp.where(row // rows_per_blk == col // cols_per_blk, tiled, 0.0)


def _s5_prep_body(lr_ref, li_ref, ldt_ref, lrf_ref, lif_ref, ldtf_ref, bre_ref, bim_ref, cre_ref, cim_ref,
                  bbar_ref, creb_ref, cimb_ref, lam_ref, *, nb, ns_blk, batch, gpb):
    _, _, zr, zi = _s5_discretise(lr_ref[...], li_ref[...], ldt_ref[...])
    sgrp, state = cre_ref.shape[1], bre_ref.shape[1]
    br = _block_diag_tile(bre_ref[...], gpb, sgrp, state)
    bi = _block_diag_tile(bim_ref[...], gpb, sgrp, state)
    bbar_ref[:, 0:ns_blk] = (zr * br - zi * bi).astype(BF16)
    bbar_ref[:, ns_blk:2 * ns_blk] = (zr * bi + zi * br).astype(BF16)
    creb_ref[...] = _block_diag_tile(cre_ref[...], gpb, state, sgrp).astype(BF16)
    cimb_ref[...] = _block_diag_tile(cim_ref[...], gpb, state, sgrp).astype(BF16)

    @pl.when(pl.program_id(1) == 0)
    def _():
        spb = SUBLANES // batch
        row = lax.broadcasted_iota(jnp.int32, (SUBLANES, ns_blk), 0)
        for ps in range(nb // spb):
            tiles = [jnp.zeros((SUBLANES, ns_blk), F32), jnp.zeros((SUBLANES, ns_blk), F32)]
            for ci in range(spb):
                cb = ps * spb + ci
                lb = _s5_discretise(lrf_ref[cb], lif_ref[cb], ldtf_ref[cb])[0:2]
                for part in range(2):
                    blk = jnp.broadcast_to(lb[part], (SUBLANES, ns_blk))
                    tiles[part] = jnp.where(row // batch == ci, blk, tiles[part])
            lam_ref[ps, 0] = tiles[0]
            lam_ref[ps, 1] = tiles[1]


def _s5_prep(lr, li, ldt, bre_t, bim_t, cre_t, cim_t, *, batch, gpb):
    depth, nb, kin, state = bre_t.shape
    ns_blk = gpb * state
    assert SUBLANES % batch == 0 and nb % (SUBLANES // batch) == 0
    assert state & (state - 1) == 0 and (kin // gpb) & (kin // gpb - 1) == 0
    npass = nb // (SUBLANES // batch)
    blk_row = pl.BlockSpec((None, None, 1, ns_blk), lambda l, c: (l, c, 0, 0))
    all_rows = pl.BlockSpec((None, nb, 1, ns_blk), lambda l, c: (l, 0, 0, 0))
    b_in = pl.BlockSpec((None, None, kin, state), lambda l, c: (l, c, 0, 0))
    c_in = pl.BlockSpec((None, None, ns_blk, kin // gpb), lambda l, c: (l, c, 0, 0))
    c_out = pl.BlockSpec((None, None, ns_blk, kin), lambda l, c: (l, c, 0, 0))
    return pl.pallas_call(
        functools.partial(_s5_prep_body, nb=nb, ns_blk=ns_blk, batch=batch, gpb=gpb),
        grid=(depth, nb),
        in_specs=[blk_row, blk_row, blk_row, all_rows, all_rows, all_rows, b_in, b_in, c_in, c_in],
        out_specs=[
            pl.BlockSpec((None, None, kin, 2 * ns_blk), lambda l, c: (l, c, 0, 0)),
            c_out, c_out,
            pl.BlockSpec((None, npass, 2, SUBLANES, ns_blk), lambda l, c: (l, 0, 0, 0, 0)),
        ],
        out_shape=[
            jax.ShapeDtypeStruct((depth, nb, kin, 2 * ns_blk), BF16),
            jax.ShapeDtypeStruct((depth, nb, ns_blk, kin), BF16),
            jax.ShapeDtypeStruct((depth, nb, ns_blk, kin), BF16),
            jax.ShapeDtypeStruct((depth, npass, 2, SUBLANES, ns_blk), F32),
        ],
        compiler_params=_cparams(("arbitrary", "arbitrary")),
        name="s5_prep",
    )(lr, li, ldt, lr, li, ldt, bre_t, bim_t, cre_t, cim_t)


def _s5_body(l_ref, u_ref, bbar_ref, lam_ref, cre_ref, cim_ref, d_ref, gw_ref, gb_ref, o_ref,
             buf, y_ref, carry_ref, *, nb, kin, ns_blk, tc, batch):
    t = pl.program_id(0)
    width = nb * kin
    pitch = tc + SUBLANES
    nsl = ns_blk // LANES
    spb = SUBLANES // batch

    @pl.when(t == 0)
    def _():
        carry_ref[...] = jnp.zeros_like(carry_ref)

    u2 = u_ref[...].reshape(batch * tc, width)
    for ps in range(nb // spb):
        for ci in range(spb):
            cb = ps * spb + ci
            bu = jnp.dot(u2[:, cb * kin:(cb + 1) * kin].astype(BF16), bbar_ref[cb], preferred_element_type=F32)
            for b in range(batch):
                r0 = (ci * batch + b) * pitch
                for j in range(2 * nsl):
                    buf[j, r0:r0 + tc, :] = bu[b * tc:(b + 1) * tc, j * LANES:(j + 1) * LANES]

        def scan_step(i, carry, ps=ps):
            new = []
            for j in range(nsl):
                ar = lam_ref[ps, 0, :, j * LANES:(j + 1) * LANES]
                ai = lam_ref[ps, 1, :, j * LANES:(j + 1) * LANES]
                sr, si = carry[j], carry[nsl + j]
                xr = buf[j, pl.ds(i, SUBLANES, stride=pitch), :]
                xi = buf[nsl + j, pl.ds(i, SUBLANES, stride=pitch), :]
                nr = ar * sr - ai * si + xr
                ni = ar * si + ai * sr + xi
                buf[j, pl.ds(i, SUBLANES, stride=pitch), :] = nr
                buf[nsl + j, pl.ds(i, SUBLANES, stride=pitch), :] = ni
                new.append((nr, ni))
            return tuple(x[0] for x in new) + tuple(x[1] for x in new)

        init = tuple(carry_ref[ps, j] for j in range(2 * nsl))
        fin = lax.fori_loop(0, tc, scan_step, init, unroll=8)
        for j in range(2 * nsl):
            carry_ref[ps, j] = fin[j]

        for ci in range(spb):
            cb = ps * spb + ci

            def gather(part, ci=ci):
                return jnp.concatenate(
                    [jnp.concatenate([buf[part * nsl + j, (ci * batch + b) * pitch:(ci * batch + b) * pitch + tc, :]
                                      for j in range(nsl)], axis=1) for b in range(batch)], axis=0).astype(BF16)

            y = (jnp.dot(gather(0), cre_ref[cb], preferred_element_type=F32)
                 - jnp.dot(gather(1), cim_ref[cb], preferred_element_type=F32)
                 + d_ref[:, cb * kin:(cb + 1) * kin] * u2[:, cb * kin:(cb + 1) * kin].astype(F32))
            y_ref[:, cb * kin:(cb + 1) * kin] = y

    y = jax.nn.gelu(y_ref[...], approximate=True)
    gate = jnp.dot(y.astype(BF16), gw_ref[...], preferred_element_type=F32) + gb_ref[...]
    o_ref[...] = (y * jax.nn.sigmoid(gate)).astype(o_ref.dtype).reshape(batch, tc, width)


def _s5(l, proj3, bbar, lam, cre_bd, cim_bd, dskip, glu_w, glu_b):
    batch, seq, _ = proj3.shape
    depth, nb, kin, ns2 = bbar.shape
    ns_blk = ns2 // 2
    width = nb * kin
    npass = lam.shape[1]
    tc = _pick(seq, (S5_TC, 64))
    return pl.pallas_call(
        functools.partial(_s5_body, nb=nb, kin=kin, ns_blk=ns_blk, tc=tc, batch=batch),
        grid_spec=pltpu.PrefetchScalarGridSpec(
            num_scalar_prefetch=1,
            grid=(seq // tc,),
            in_specs=[
                pl.BlockSpec((batch, tc, width), lambda t, l: (0, t, 0)),
                pl.BlockSpec((None, nb, kin, ns2), lambda t, l: (l[0], 0, 0, 0)),
                pl.BlockSpec((None, npass, 2, SUBLANES, ns_blk), lambda t, l: (l[0], 0, 0, 0, 0)),
                pl.BlockSpec((None, nb, ns_blk, kin), lambda t, l: (l[0], 0, 0, 0)),
                pl.BlockSpec((None, nb, ns_blk, kin), lambda t, l: (l[0], 0, 0, 0)),
                pl.BlockSpec((None, 1, width), lambda t, l: (l[0], 0, 0)),
                pl.BlockSpec((None, width, width), lambda t, l: (l[0], 0, 0)),
                pl.BlockSpec((None, 1, width), lambda t, l: (l[0], 0, 0)),
            ],
            out_specs=pl.BlockSpec((batch, tc, width), lambda t, l: (0, t, 0)),
            scratch_shapes=[
                pltpu.VMEM((2 * ns_blk // LANES, SUBLANES * (tc + SUBLANES), LANES), F32),
                pltpu.VMEM((batch * tc, width), F32),
                pltpu.VMEM((npass, 2 * ns_blk // LANES, SUBLANES, LANES), F32),
            ],
        ),
        out_shape=jax.ShapeDtypeStruct((batch, seq, width), ACT),
        compiler_params=_cparams(("arbitrary",)),
        name="s5",
    )(l, proj3, bbar, lam, cre_bd, cim_bd, dskip, glu_w, glu_b)


def _gdn_prepare(hh, head, src_refs, cw_refs, bg_s, xpad, qs, ks, vs, gs, bs, *, seq, heads, hd):
    rb = GDN_GROUP
    pad = SUBLANES
    lanes = slice(hh * hd, (hh + 1) * hd)
    for src, cw, dst, mode in zip(src_refs, cw_refs, (qs, ks, vs), "qkv"):
        xpad[pad:pad + seq, :] = src[:, lanes].astype(F32)
        w = cw[:, lanes]
        for r0 in range(0, seq, rb):
            acc = w[0:1, :] * xpad[r0 + pad - 3:r0 + pad - 3 + rb, :]
            for j in range(1, CONV_K):
                acc = acc + w[j:j + 1, :] * xpad[r0 + pad - 3 + j:r0 + pad - 3 + j + rb, :]
            y = _silu(acc)
            if mode != "v":
                y = y * lax.rsqrt(jnp.sum(y * y, axis=-1, keepdims=True) + L2_EPS)
            if mode == "q":
                y = y * (hd ** -0.5)
            dst[r0:r0 + rb, :] = y

    lane = lax.broadcasted_iota(jnp.int32, (rb, LANES), 1)
    for r0 in range(0, seq, rb):
        bg = bg_s[r0:r0 + rb, :]
        gcol = jnp.sum(jnp.where(lane == heads + head, bg, 0.0), axis=-1, keepdims=True)
        bcol = jnp.sum(jnp.where(lane == head, bg, 0.0), axis=-1, keepdims=True)
        gs[r0:r0 + rb, :] = jnp.broadcast_to(gcol, (rb, hd))
        bs[r0:r0 + rb, :] = jnp.broadcast_to(bcol, (rb, hd))


def _gdn_gate_terms(sm_ref, alog_ref, dtb_ref, bg_s, *, seq, heads):
    rb = GDN_GROUP
    lane = lax.broadcasted_iota(jnp.int32, (rb, LANES), 1)
    neg_a = -jnp.exp(alog_ref[...])
    dtb = dtb_ref[...]
    for r0 in range(0, seq, rb):
        s = sm_ref[r0:r0 + rb, :]
        xa = s + dtb
        sp = jnp.maximum(xa, 0.0) + jnp.log(1.0 + jnp.exp(-jnp.abs(xa)))
        bg_s[r0:r0 + rb, :] = jnp.where(lane < heads, jax.nn.sigmoid(s), neg_a * sp)


def _gdn_solve(hh, r0s, qs, ks, vs, gs, bs, kq_s, n_s, oc_s, gl_s, consts, *, hd):
    rb = GDN_GROUP
    ch = GDN_CHUNK
    causal_ch, strict_ch, cum_m, eye_f, m16, m32_16, m64_32 = consts
    reps = rb // hd
    cpg = rb // ch
    gi = range(len(r0s))

    q = [qs[r0:r0 + rb, :] for r0 in r0s]
    k = [ks[r0:r0 + rb, :] for r0 in r0s]
    v = [vs[r0:r0 + rb, :] for r0 in r0s]
    bb = [bs[r0:r0 + rb, :] for r0 in r0s]
    gcl = [_dot01(cum_m, gs[r0:r0 + rb, :]) for r0 in r0s]
    gc = [t[0:rb, :] for t in gcl]
    gl = [t[rb:2 * rb, :] for t in gcl]
    eg = [jnp.exp(t) for t in gc]
    gcw = [t if reps == 1 else jnp.concatenate([t] * reps, axis=1) for t in gc]
    e = [jnp.exp(jnp.where(causal_ch, t - t.T, 0.0)) for t in gcw]
    kb = [k[i] * bb[i] for i in gi]
    kkqk = [lax.dot_general(jnp.concatenate([kb[i], q[i]], axis=0).astype(BF16), k[i].astype(BF16),
                            (((1,), (1,)), ((), ())), preferred_element_type=F32) for i in gi]
    a = [jnp.where(strict_ch, kkqk[i][0:rb, :] * e[i], 0.0) for i in gi]
    attn = [jnp.where(causal_ch, kkqk[i][rb:2 * rb, :] * e[i], 0.0).astype(BF16) for i in gi]
    p = [-(t * m16) for t in a]
    tinv = [eye_f + t for t in p]
    for _ in range(3):
        p = [_bdot(t, t) for t in p]
        tinv = [tinv[i] + _bdot(tinv[i], p[i]) for i in gi]
    for msk in (m32_16, m64_32):
        low = [_bdot(a[i] * msk, tinv[i]) for i in gi]
        tinv = [tinv[i] - _bdot(tinv[i], low[i]) for i in gi]
    wu = [_bdot(tinv[i], jnp.concatenate([kb[i] * eg[i], v[i] * bb[i]], axis=1)).astype(BF16) for i in gi]
    aw = [jnp.dot(attn[i], wu[i], preferred_element_type=F32) for i in gi]
    qe = [(q[i] * eg[i] - aw[i][:, 0:hd]).astype(BF16) for i in gi]
    kd = [(k[i] * jnp.exp(gl[i] - gc[i])).astype(BF16) for i in gi]
    egl = [jnp.exp(t) for t in gl]
    for i, r0 in enumerate(r0s):
        oc_s[hh, r0:r0 + rb, :] = aw[i][:, hd:2 * hd]
    for c in range(cpg):
        rows = slice(c * ch, (c + 1) * ch)
        kn = [lax.dot_general(kd[i][rows, :], wu[i][rows, :], (((0,), (0,)), ((), ())),
                              preferred_element_type=F32) for i in gi]
        for i, r0 in enumerate(r0s):
            n = r0 // ch + c
            kq_s[hh, n, 0:hd, :] = kn[i][:, 0:hd].astype(BF16)
            kq_s[hh, n, hd:hd + ch, :] = qe[i][rows, :]
            n_s[hh, n] = kn[i][:, hd:2 * hd]
            gl_s[hh, n] = egl[i][c * ch:c * ch + SUBLANES, :]


def _gdn_body(l_ref, q_ref, k_ref, v_ref, z_ref, sm_ref, cwq_ref, cwk_ref, cwv_ref, alog_ref, dtb_ref,
              nw_ref, o_ref, xpad, bg_s, qs, ks, vs, gs, bs, kq_s, n_s, oc_s, gl_s, *, seq, heads, hd, hp):
    rb = GDN_GROUP
    ch = GDN_CHUNK
    head0 = pl.program_id(1) * hp

    xpad[0:SUBLANES, :] = jnp.zeros((SUBLANES, hd), F32)
    _gdn_gate_terms(sm_ref, alog_ref, dtb_ref, bg_s, seq=seq, heads=heads)
    row = lax.broadcasted_iota(jnp.int32, (rb, rb), 0)
    col = lax.broadcasted_iota(jnp.int32, (rb, rb), 1)

    def same(blk):
        return (row // blk) == (col // blk)

    def as_f(m):
        return jnp.where(m, 1.0, 0.0).astype(F32)

    same_ch = same(ch)
    causal_ch = same_ch & (row >= col)
    strict_ch = same_ch & (row > col)
    cum_m = jnp.concatenate([as_f(causal_ch), as_f(same_ch)], axis=0).astype(BF16)
    m16 = as_f(same(16))
    m32 = as_f(same(32))
    consts = (causal_ch, strict_ch, cum_m, as_f(row == col), m16, m32 - m16, as_f(same_ch) - m32)
    step_rows = rb * GDN_INTERLEAVE
    for hh in range(hp):
        _gdn_prepare(hh, head0 + hh, (q_ref, k_ref, v_ref), (cwq_ref, cwk_ref, cwv_ref), bg_s, xpad, qs, ks, vs,
                     gs, bs, seq=seq, heads=heads, hd=hd)
        for base in range(0, seq, step_rows):
            r0s = list(range(base, min(base + step_rows, seq), rb))
            _gdn_solve(hh, r0s, qs, ks, vs, gs, bs, kq_s, n_s, oc_s, gl_s, consts, hd=hd)

    nw = nw_ref[...]

    def step(n, states):
        r0 = pl.multiple_of(n * ch, ch)
        new = []
        for hh in range(hp):
            state = states[hh]
            r = jnp.dot(kq_s[hh, n], state.astype(BF16), preferred_element_type=F32)
            out = r[hd:hd + ch, :] + oc_s[hh, pl.ds(r0, ch), :]
            new.append(state * gl_s[hh, n][0:1, :] + (n_s[hh, n] - r[0:hd, :]))
            o = out * lax.rsqrt(jnp.mean(out * out, axis=-1, keepdims=True) + RMS_EPS) * nw
            zz = z_ref[pl.ds(r0, ch), hh * hd:(hh + 1) * hd].astype(F32)
            o_ref[pl.ds(r0, ch), hh * hd:(hh + 1) * hd] = (o * _silu(zz)).astype(o_ref.dtype)
        return tuple(new)

    lax.fori_loop(0, seq // ch, step, tuple(jnp.zeros((hd, hd), F32) for _ in range(hp)), unroll=2)


def _gdn(l, proj, small, conv_w, alog, dtb, norm_w, *, batch, seq, heads, hd, col0):
    m = batch * seq
    hp = GDN_HEADS_PER_STEP
    cb = col0 // hp
    hb = heads // hp
    nch = seq // GDN_CHUNK

    def act_spec(off):
        return pl.BlockSpec((seq, hp * hd), lambda b, h, l, off=off: (b, off + h))

    def cw_spec(off):
        return pl.BlockSpec((None, CONV_K, hp * hd), lambda b, h, l, off=off: (l[0], 0, off + h))

    row_spec = pl.BlockSpec((None, 1, LANES), lambda b, h, l: (l[0], 0, 0))
    return pl.pallas_call(
        functools.partial(_gdn_body, seq=seq, heads=heads, hd=hd, hp=hp),
        grid_spec=pltpu.PrefetchScalarGridSpec(
            num_scalar_prefetch=1,
            grid=(batch, hb),
            in_specs=[
                act_spec(cb), act_spec(cb + hb), act_spec(cb + 2 * hb), act_spec(cb + 3 * hb),
                pl.BlockSpec((seq, LANES), lambda b, h, l: (b, 0), pipeline_mode=pl.Buffered(1)),
                cw_spec(0), cw_spec(hb), cw_spec(2 * hb),
                row_spec, row_spec,
                pl.BlockSpec((None, 1, hd), lambda b, h, l: (l[0], 0, 0)),
            ],
            out_specs=pl.BlockSpec((seq, hp * hd), lambda b, h, l: (b, h)),
            scratch_shapes=(
                [pltpu.VMEM((seq + SUBLANES, hd), F32), pltpu.VMEM((seq, LANES), F32)]
                + [pltpu.VMEM((seq, hd), F32) for _ in range(5)]
                + [pltpu.VMEM((hp, nch, hd + GDN_CHUNK, hd), BF16),
                   pltpu.VMEM((hp, nch, hd, hd), F32),
                   pltpu.VMEM((hp, seq, hd), F32),
                   pltpu.VMEM((hp, nch, SUBLANES, hd), F32)]
            ),
        ),
        out_shape=jax.ShapeDtypeStruct((m, heads * hd), ACT),
        compiler_params=_cparams(("parallel", "parallel")),
        name="gdn",
    )(l, proj, proj, proj, proj, small, conv_w, conv_w, conv_w, alog, dtb, norm_w)


def _merge_body(l_ref, x_ref, ys_ref, yg_ref, gs0_ref, gs1_ref, gs2_ref, gd0_ref, gd1_ref, gd2_ref, wbs_ref,
                wbg_ref, wo_ref, g_ref, b_ref, o_ref, *, alpha):
    ys = ys_ref[...].astype(BF16)
    yg = yg_ref[...].astype(BF16)
    acc = None
    c0 = 0
    for gsr, gdr in ((gs0_ref, gd0_ref), (gs1_ref, gd1_ref), (gs2_ref, gd2_ref)):
        sl = slice(c0, c0 + gsr.shape[1])
        c0 += gsr.shape[1]
        mix = (jax.nn.sigmoid(gsr[...].astype(F32)) * jnp.dot(ys, wbs_ref[:, sl], preferred_element_type=F32)
               + jax.nn.sigmoid(gdr[...].astype(F32)) * jnp.dot(yg, wbg_ref[:, sl], preferred_element_type=F32))
        part = jnp.dot(mix.astype(BF16), wo_ref[sl, :], preferred_element_type=F32)
        acc = part if acc is None else acc + part
    y = alpha * x_ref[...] + acc
    o_ref[...] = _layer_norm(y, g_ref[...], b_ref[...])


def _merge(l, x, y_ssm, y_gdn, proj, tail, wbs, wbg, wo, g, b, *, frame0, alpha):
    m, d = x.shape
    hw = d // 2
    dp = d + LANES
    ws = y_ssm.shape[1]
    wg = y_gdn.shape[1]
    tm = _pick(m, (512, 256, 128))
    assert frame0 % hw == 0 and (frame0 + d) % LANES == 0 and frame0 + 2 * d == proj.shape[1]

    def half_spec(col):
        return pl.BlockSpec((tm, hw), lambda i, l: (i, col // hw))

    def lane_spec(col):
        return pl.BlockSpec((tm, LANES), lambda i, l: (i, col // LANES))

    def resident(shape):
        return pl.BlockSpec((None,) + shape, lambda i, l: (l[0], 0, 0), pipeline_mode=pl.Buffered(1))

    return pl.pallas_call(
        functools.partial(_merge_body, alpha=alpha),
        grid_spec=pltpu.PrefetchScalarGridSpec(
            num_scalar_prefetch=1,
            grid=(m // tm,),
            in_specs=[
                pl.BlockSpec((tm, d), lambda i, l: (i, 0)),
                pl.BlockSpec((tm, ws), lambda i, l: (i, 0)),
                pl.BlockSpec((tm, wg), lambda i, l: (i, 0)),
                half_spec(frame0), half_spec(frame0 + hw), lane_spec(frame0 + d),
                half_spec(frame0 + d), half_spec(frame0 + d + hw), pl.BlockSpec((tm, LANES), lambda i, l: (i, 0)),
                resident((ws, dp)), resident((wg, dp)), resident((dp, d)),
                pl.BlockSpec((None, 1, d), lambda i, l: (l[0], 0, 0)),
                pl.BlockSpec((None, 1, d), lambda i, l: (l[0], 0, 0)),
            ],
            out_specs=pl.BlockSpec((tm, d), lambda i, l: (i, 0)),
        ),
        out_shape=jax.ShapeDtypeStruct((m, d), F32),
        compiler_params=_cparams(("parallel",)),
        name="merge",
    )(l, x, y_ssm, y_gdn, proj, proj, proj, proj, proj, tail, wbs, wbg, wo, g, b)


def kernel(x, ffn1_w_gu, ffn1_w_down, ln1_g, ln1_b, w_in, conv_w, ssm_a_re, ssm_a_im, ssm_log_dt, ssm_b_re,
           ssm_b_im, ssm_c_re, ssm_c_im, ssm_d, glu_w, glu_b, gdn_a_log, gdn_dt_bias, gdn_norm_w, w_br_ssm,
           w_br_gdn, w_out, ln2_g, ln2_b, ffn2_w_gu, ffn2_w_down, ln3_g, ln3_b):
    batch, seq, d = x.shape
    depth = ffn1_w_gu.shape[0]
    alpha = (2.0 * depth) ** 0.25
    groups, state, sgrp = ssm_b_re.shape[1:]
    ssm_width = groups * sgrp
    heads = gdn_a_log.shape[1]
    hd = gdn_norm_w.shape[1]
    gdn_width = heads * hd
    n_act = ssm_width + 4 * gdn_width
    n_small = 2 * heads

    def row3(t):
        return t.reshape(depth, 1, -1).astype(F32)

    n_full = n_act + 2 * d
    assert w_in.shape[2] - n_full == n_small < LANES

    def lane_pad(t):
        return jnp.pad(t, ((0, 0), (0, 0), (0, LANES - t.shape[2]))).astype(BF16)

    w_small = lane_pad(w_in[:, :, n_act:n_act + n_small])
    w_tail = lane_pad(w_in[:, :, n_full:])
    frame_pad = (n_small, LANES - n_small)
    wbs = jnp.pad(w_br_ssm, ((0, 0), (0, 0), frame_pad)).astype(BF16)
    wbg = jnp.pad(w_br_gdn, ((0, 0), (0, 0), frame_pad)).astype(BF16)
    wo = jnp.pad(w_out, ((0, 0), frame_pad, (0, 0))).astype(BF16)
    gluw = glu_w.astype(BF16)

    gpb = S5_GROUPS_PER_BLOCK
    nb = groups // gpb

    def b_stack(t):
        return jnp.swapaxes(t, 2, 3).reshape(depth, nb, gpb * sgrp, state).astype(F32)

    def c_stack(t):
        return jnp.swapaxes(t, 2, 3).reshape(depth, nb, gpb * state, sgrp).astype(F32)

    def blk_rows(t):
        return t.reshape(depth, nb, 1, gpb * state).astype(F32)

    bbar, cre_bd, cim_bd, lam = _s5_prep(blk_rows(ssm_a_re), blk_rows(ssm_a_im),
                                         blk_rows(jnp.repeat(ssm_log_dt, state, axis=1)),
                                         b_stack(ssm_b_re), b_stack(ssm_b_im), c_stack(ssm_c_re),
                                         c_stack(ssm_c_im), batch=batch, gpb=gpb)

    alog = jnp.pad(gdn_a_log, ((0, 0), (heads, LANES - 2 * heads))).reshape(depth, 1, LANES)
    dtb = jnp.pad(gdn_dt_bias, ((0, 0), (heads, LANES - 2 * heads))).reshape(depth, 1, LANES)

    def layer(xc, li_):
        l = jnp.reshape(li_, (1,)).astype(jnp.int32)
        xc = _ffn(l, xc, ffn1_w_gu, ffn1_w_down, row3(ln1_g), row3(ln1_b), alpha=alpha)
        proj, small, tail = _inproj(l, xc, w_in, w_small, w_tail, n_full=n_full)
        y_ssm = _s5(l, proj.reshape(batch, seq, -1), bbar, lam, cre_bd, cim_bd, row3(ssm_d), gluw, row3(glu_b))
        y_ssm = y_ssm.reshape(batch * seq, ssm_width)
        y_gdn = _gdn(l, proj, small, conv_w, alog, dtb, row3(gdn_norm_w), batch=batch, seq=seq, heads=heads,
                     hd=hd, col0=ssm_width // hd)
        xc = _merge(l, xc, y_ssm, y_gdn, proj, tail, wbs, wbg, wo, row3(ln2_g), row3(ln2_b),
                    frame0=n_act, alpha=alpha)
        xc = _ffn(l, xc, ffn2_w_gu, ffn2_w_down, row3(ln3_g), row3(ln3_b), alpha=alpha)
        return xc, None

    out, _ = lax.scan(layer, x.reshape(batch * seq, d), jnp.arange(depth, dtype=jnp.int32))
    return out.reshape(batch, seq, d)
```

```python
import functools

import jax
import jax.numpy as jnp
from jax import lax
from jax.experimental import pallas as pl
from jax.experimental.pallas import tpu as pltpu

F32 = jnp.float32
BF16 = jnp.bfloat16
ACT = BF16

LN_EPS = 1e-5
RMS_EPS = 1e-6
L2_EPS = 1e-6
CONV_K = 4
GDN_CHUNK = 64
GDN_GROUP = 128
GDN_INTERLEAVE = 16
GDN_HEADS_PER_STEP = 4
S5_TC = 256
S5_GROUPS_PER_BLOCK = 16
LANES = 128
SUBLANES = 8
VMEM_LIMIT = 56 * 1024 * 1024


def _cparams(sem):
    return pltpu.CompilerParams(dimension_semantics=sem, vmem_limit_bytes=VMEM_LIMIT)


def _pick(n, cands):
    for c in cands:
        if n % c == 0:
            return c
    raise ValueError(f"no tile in {cands} divides {n}")


def _layer_norm(y, g, b):
    mu = jnp.mean(y, axis=-1, keepdims=True)
    yc = y - mu
    var = jnp.mean(yc * yc, axis=-1, keepdims=True)
    return yc * lax.rsqrt(var + LN_EPS) * g + b


def _silu(x):
    return x * jax.nn.sigmoid(x)


def _cmul(ar, ai, br, bi):
    return ar * br - ai * bi, ar * bi + ai * br


def _bdot(a, b):
    return jnp.dot(a.astype(BF16), b.astype(BF16), preferred_element_type=F32)


def _dot01(m01, x):
    x1 = x.astype(BF16)
    r1 = x - x1.astype(F32)
    x2 = r1.astype(BF16)
    x3 = (r1 - x2.astype(F32)).astype(BF16)
    return (jnp.dot(m01, x1, preferred_element_type=F32) + jnp.dot(m01, x2, preferred_element_type=F32)
            + jnp.dot(m01, x3, preferred_element_type=F32))


def _ffn_body(l_ref, x_ref, wg_ref, wu_ref, wd_ref, g_ref, b_ref, o_ref, xb_ref, *, nf, alpha):
    j = pl.program_id(1)

    @pl.when(j == 0)
    def _():
        xb_ref[...] = x_ref[...].astype(BF16)
        o_ref[...] = jnp.zeros_like(o_ref)

    xb = xb_ref[...]
    gate = jnp.dot(xb, wg_ref[...].astype(BF16), preferred_element_type=F32)
    up = jnp.dot(xb, wu_ref[...].astype(BF16), preferred_element_type=F32)
    h = (_silu(gate) * up).astype(BF16)
    o_ref[...] += jnp.dot(h, wd_ref[...].astype(BF16), preferred_element_type=F32)

    @pl.when(j == nf - 1)
    def _():
        y = alpha * x_ref[...] + 0.5 * o_ref[...]
        o_ref[...] = _layer_norm(y, g_ref[...], b_ref[...])


def _ffn(l, x, wgu, wd, g, b, *, alpha):
    m, d = x.shape
    f = wd.shape[1]
    tm = _pick(m, (1024, 512, 256, 128))
    tf = _pick(f, (256, 128))
    nf = f // tf
    return pl.pallas_call(
        functools.partial(_ffn_body, nf=nf, alpha=alpha),
        grid_spec=pltpu.PrefetchScalarGridSpec(
            num_scalar_prefetch=1,
            grid=(m // tm, nf),
            in_specs=[
                pl.BlockSpec((tm, d), lambda i, j, l: (i, 0), pipeline_mode=pl.Buffered(1)),
                pl.BlockSpec((None, d, tf), lambda i, j, l: (l[0], 0, j)),
                pl.BlockSpec((None, d, tf), lambda i, j, l: (l[0], 0, j + nf)),
                pl.BlockSpec((None, tf, d), lambda i, j, l: (l[0], j, 0)),
                pl.BlockSpec((None, 1, d), lambda i, j, l: (l[0], 0, 0)),
                pl.BlockSpec((None, 1, d), lambda i, j, l: (l[0], 0, 0)),
            ],
            out_specs=pl.BlockSpec((tm, d), lambda i, j, l: (i, 0)),
            scratch_shapes=[pltpu.VMEM((tm, d), BF16)],
        ),
        out_shape=jax.ShapeDtypeStruct((m, d), F32),
        compiler_params=_cparams(("parallel", "arbitrary")),
        name="ffn",
    )(l, x, wgu, wgu, wd, g, b)


def _inproj_body(l_ref, x_ref, wa_ref, wg_ref, ws_ref, o_ref, os_ref, xb_ref, *, na):
    j = pl.program_id(1)

    @pl.when(j == 0)
    def _():
        xb_ref[...] = x_ref[...].astype(BF16)
        os_ref[...] = jnp.dot(xb_ref[...], ws_ref[...], preferred_element_type=F32)

    @pl.when(j < na)
    def _():
        o_ref[...] = jnp.dot(xb_ref[...], wa_ref[...], preferred_element_type=F32).astype(o_ref.dtype)

    @pl.when(j >= na)
    def _():
        o_ref[...] = jnp.dot(xb_ref[...], wg_ref[...], preferred_element_type=F32).astype(o_ref.dtype)


def _inproj(l, x, w_act, w_gate, w_small):
    m, d = x.shape
    n_a, n_g = w_act.shape[2], w_gate.shape[2]
    n = n_a + n_g
    tm = _pick(m, (1024, 512, 256, 128))
    tn = next(c for c in (1024, 512, 256, 128) if n_a % c == 0 and n_g % c == 0)
    na = n_a // tn
    return pl.pallas_call(
        functools.partial(_inproj_body, na=na),
        grid_spec=pltpu.PrefetchScalarGridSpec(
            num_scalar_prefetch=1,
            grid=(m // tm, n // tn),
            in_specs=[
                pl.BlockSpec((tm, d), lambda i, j, l: (i, 0)),
                pl.BlockSpec((None, d, tn), lambda i, j, l: (l[0], 0, jnp.minimum(j, na - 1))),
                pl.BlockSpec((None, d, tn), lambda i, j, l: (l[0], 0, jnp.maximum(j - na, 0))),
                pl.BlockSpec((None, d, LANES), lambda i, j, l: (l[0], 0, 0)),
            ],
            out_specs=[
                pl.BlockSpec((tm, tn), lambda i, j, l: (i, j)),
                pl.BlockSpec((tm, LANES), lambda i, j, l: (i, 0)),
            ],
            scratch_shapes=[pltpu.VMEM((tm, d), BF16)],
        ),
        out_shape=[jax.ShapeDtypeStruct((m, n), ACT), jax.ShapeDtypeStruct((m, LANES), F32)],
        compiler_params=_cparams(("parallel", "arbitrary")),
        name="inproj",
    )(l, x, w_act, w_gate, w_small)


def _s5_discretise(lr, li, ldt):
    dt = jnp.exp(ldt)
    mag = jnp.exp(lr * dt)
    lbr = mag * jnp.cos(li * dt)
    lbi = mag * jnp.sin(li * dt)
    den = lr * lr + li * li
    zr, zi = _cmul(lbr - 1.0, lbi, lr / den, -li / den)
    return lbr, lbi, zr, zi


def _block_diag_tile(x, reps, rows_per_blk, cols_per_blk):
    c = cols_per_blk
    kk = lax.broadcasted_iota(jnp.int32, (c, reps * c), 0)
    nn = lax.broadcasted_iota(jnp.int32, (c, reps * c), 1)
    rep = jnp.where((nn & (c - 1)) == kk, 1.0, 0.0).astype(BF16)
    x1 = x.astype(BF16)
    r1 = x - x1.astype(F32)
    x2 = r1.astype(BF16)
    x3 = (r1 - x2.astype(F32)).astype(BF16)
    tiled = (jnp.dot(x1, rep, preferred_element_type=F32) + jnp.dot(x2, rep, preferred_element_type=F32)
             + jnp.dot(x3, rep, preferred_element_type=F32))
    row = lax.broadcasted_iota(jnp.int32, tiled.shape, 0)
    col = lax.broadcasted_iota(jnp.int32, tiled.shape, 1)
    return jnp.where(row // rows_per_blk == col // cols_per_blk, tiled, 0.0)


def _s5_prep_body(lr_ref, li_ref, ldt_ref, lrf_ref, lif_ref, ldtf_ref, bre_ref, bim_ref, cre_ref, cim_ref,
                  bbar_ref, creb_ref, cimb_ref, lam_ref, *, nb, ns_blk, batch, gpb):
    _, _, zr, zi = _s5_discretise(lr_ref[...], li_ref[...], ldt_ref[...])
    sgrp, state = cre_ref.shape[1], bre_ref.shape[1]
    br = _block_diag_tile(bre_ref[...], gpb, sgrp, state)
    bi = _block_diag_tile(bim_ref[...], gpb, sgrp, state)
    bbar_ref[:, 0:ns_blk] = (zr * br - zi * bi).astype(BF16)
    bbar_ref[:, ns_blk:2 * ns_blk] = (zr * bi + zi * br).astype(BF16)
    creb_ref[...] = _block_diag_tile(cre_ref[...], gpb, state, sgrp).astype(BF16)
    cimb_ref[...] = _block_diag_tile(cim_ref[...], gpb, state, sgrp).astype(BF16)

    @pl.when(pl.program_id(1) == 0)
    def _():
        spb = SUBLANES // batch
        row = lax.broadcasted_iota(jnp.int32, (SUBLANES, ns_blk), 0)
        for ps in range(nb // spb):
            tiles = [jnp.zeros((SUBLANES, ns_blk), F32), jnp.zeros((SUBLANES, ns_blk), F32)]
            for ci in range(spb):
                cb = ps * spb + ci
                lb = _s5_discretise(lrf_ref[cb], lif_ref[cb], ldtf_ref[cb])[0:2]
                for part in range(2):
                    blk = jnp.broadcast_to(lb[part], (SUBLANES, ns_blk))
                    tiles[part] = jnp.where(row // batch == ci, blk, tiles[part])
            lam_ref[ps, 0] = tiles[0]
            lam_ref[ps, 1] = tiles[1]


def _s5_prep(lr, li, ldt, bre_t, bim_t, cre_t, cim_t, *, batch, gpb):
    depth, nb, kin, state = bre_t.shape
    ns_blk = gpb * state
    assert SUBLANES % batch == 0 and nb % (SUBLANES // batch) == 0
    assert state & (state - 1) == 0 and (kin // gpb) & (kin // gpb - 1) == 0
    npass = nb // (SUBLANES // batch)
    blk_row = pl.BlockSpec((None, None, 1, ns_blk), lambda l, c: (l, c, 0, 0))
    all_rows = pl.BlockSpec((None, nb, 1, ns_blk), lambda l, c: (l, 0, 0, 0))
    b_in = pl.BlockSpec((None, None, kin, state), lambda l, c: (l, c, 0, 0))
    c_in = pl.BlockSpec((None, None, ns_blk, kin // gpb), lambda l, c: (l, c, 0, 0))
    c_out = pl.BlockSpec((None, None, ns_blk, kin), lambda l, c: (l, c, 0, 0))
    return pl.pallas_call(
        functools.partial(_s5_prep_body, nb=nb, ns_blk=ns_blk, batch=batch, gpb=gpb),
        grid=(depth, nb),
        in_specs=[blk_row, blk_row, blk_row, all_rows, all_rows, all_rows, b_in, b_in, c_in, c_in],
        out_specs=[
            pl.BlockSpec((None, None, kin, 2 * ns_blk), lambda l, c: (l, c, 0, 0)),
            c_out, c_out,
            pl.BlockSpec((None, npass, 2, SUBLANES, ns_blk), lambda l, c: (l, 0, 0, 0, 0)),
        ],
        out_shape=[
            jax.ShapeDtypeStruct((depth, nb, kin, 2 * ns_blk), BF16),
            jax.ShapeDtypeStruct((depth, nb, ns_blk, kin), BF16),
            jax.ShapeDtypeStruct((depth, nb, ns_blk, kin), BF16),
            jax.ShapeDtypeStruct((depth, npass, 2, SUBLANES, ns_blk), F32),
        ],
        compiler_params=_cparams(("arbitrary", "arbitrary")),
        name="s5_prep",
    )(lr, li, ldt, lr, li, ldt, bre_t, bim_t, cre_t, cim_t)


def _s5_body(l_ref, u_ref, bbar_ref, lam_ref, cre_ref, cim_ref, d_ref, gw_ref, gb_ref, o_ref,
             buf, y_ref, carry_ref, *, nb, kin, ns_blk, tc, batch):
    t = pl.program_id(0)
    width = nb * kin
    pitch = tc + SUBLANES
    nsl = ns_blk // LANES
    spb = SUBLANES // batch

    @pl.when(t == 0)
    def _():
        carry_ref[...] = jnp.zeros_like(carry_ref)

    u2 = u_ref[...].reshape(batch * tc, width)
    for ps in range(nb // spb):
        for ci in range(spb):
            cb = ps * spb + ci
            bu = jnp.dot(u2[:, cb * kin:(cb + 1) * kin].astype(BF16), bbar_ref[cb], preferred_element_type=F32)
            for b in range(batch):
                r0 = (ci * batch + b) * pitch
                for j in range(2 * nsl):
                    buf[j, r0:r0 + tc, :] = bu[b * tc:(b + 1) * tc, j * LANES:(j + 1) * LANES]

        def scan_step(i, carry, ps=ps):
            new = []
            for j in range(nsl):
                ar = lam_ref[ps, 0, :, j * LANES:(j + 1) * LANES]
                ai = lam_ref[ps, 1, :, j * LANES:(j + 1) * LANES]
                sr, si = carry[j], carry[nsl + j]
                xr = buf[j, pl.ds(i, SUBLANES, stride=pitch), :]
                xi = buf[nsl + j, pl.ds(i, SUBLANES, stride=pitch), :]
                nr = ar * sr - ai * si + xr
                ni = ar * si + ai * sr + xi
                buf[j, pl.ds(i, SUBLANES, stride=pitch), :] = nr
                buf[nsl + j, pl.ds(i, SUBLANES, stride=pitch), :] = ni
                new.append((nr, ni))
            return tuple(x[0] for x in new) + tuple(x[1] for x in new)

        init = tuple(carry_ref[ps, j] for j in range(2 * nsl))
        fin = lax.fori_loop(0, tc, scan_step, init, unroll=8)
        for j in range(2 * nsl):
            carry_ref[ps, j] = fin[j]

        for ci in range(spb):
            cb = ps * spb + ci

            def gather(part, ci=ci):
                return jnp.concatenate(
                    [jnp.concatenate([buf[part * nsl + j, (ci * batch + b) * pitch:(ci * batch + b) * pitch + tc, :]
                                      for j in range(nsl)], axis=1) for b in range(batch)], axis=0).astype(BF16)

            y = (jnp.dot(gather(0), cre_ref[cb], preferred_element_type=F32)
                 - jnp.dot(gather(1), cim_ref[cb], preferred_element_type=F32)
                 + d_ref[:, cb * kin:(cb + 1) * kin] * u2[:, cb * kin:(cb + 1) * kin].astype(F32))
            y_ref[:, cb * kin:(cb + 1) * kin] = y

    y = jax.nn.gelu(y_ref[...], approximate=True)
    gate = jnp.dot(y.astype(BF16), gw_ref[...], preferred_element_type=F32) + gb_ref[...]
    o_ref[...] = (y * jax.nn.sigmoid(gate)).astype(o_ref.dtype).reshape(batch, tc, width)


def _s5(l, proj3, bbar, lam, cre_bd, cim_bd, dskip, glu_w, glu_b):
    batch, seq, _ = proj3.shape
    depth, nb, kin, ns2 = bbar.shape
    ns_blk = ns2 // 2
    width = nb * kin
    npass = lam.shape[1]
    tc = _pick(seq, (S5_TC, 64))
    return pl.pallas_call(
        functools.partial(_s5_body, nb=nb, kin=kin, ns_blk=ns_blk, tc=tc, batch=batch),
        grid_spec=pltpu.PrefetchScalarGridSpec(
            num_scalar_prefetch=1,
            grid=(seq // tc,),
            in_specs=[
                pl.BlockSpec((batch, tc, width), lambda t, l: (0, t, 0)),
                pl.BlockSpec((None, nb, kin, ns2), lambda t, l: (l[0], 0, 0, 0)),
                pl.BlockSpec((None, npass, 2, SUBLANES, ns_blk), lambda t, l: (l[0], 0, 0, 0, 0)),
                pl.BlockSpec((None, nb, ns_blk, kin), lambda t, l: (l[0], 0, 0, 0)),
                pl.BlockSpec((None, nb, ns_blk, kin), lambda t, l: (l[0], 0, 0, 0)),
                pl.BlockSpec((None, 1, width), lambda t, l: (l[0], 0, 0)),
                pl.BlockSpec((None, width, width), lambda t, l: (l[0], 0, 0)),
                pl.BlockSpec((None, 1, width), lambda t, l: (l[0], 0, 0)),
            ],
            out_specs=pl.BlockSpec((batch, tc, width), lambda t, l: (0, t, 0)),
            scratch_shapes=[
                pltpu.VMEM((2 * ns_blk // LANES, SUBLANES * (tc + SUBLANES), LANES), F32),
                pltpu.VMEM((batch * tc, width), F32),
                pltpu.VMEM((npass, 2 * ns_blk // LANES, SUBLANES, LANES), F32),
            ],
        ),
        out_shape=jax.ShapeDtypeStruct((batch, seq, width), ACT),
        compiler_params=_cparams(("arbitrary",)),
        name="s5",
    )(l, proj3, bbar, lam, cre_bd, cim_bd, dskip, glu_w, glu_b)


def _gdn_prepare(hh, head, src_refs, cw_refs, bg_s, xpad, qs, ks, vs, gs, bs, *, seq, heads, hd):
    rb = GDN_GROUP
    pad = SUBLANES
    lanes = slice(hh * hd, (hh + 1) * hd)
    for src, cw, dst, mode in zip(src_refs, cw_refs, (qs, ks, vs), "qkv"):
        xpad[pad:pad + seq, :] = src[:, lanes].astype(F32)
        w = cw[:, lanes]
        for r0 in range(0, seq, rb):
            acc = w[0:1, :] * xpad[r0 + pad - 3:r0 + pad - 3 + rb, :]
            for j in range(1, CONV_K):
                acc = acc + w[j:j + 1, :] * xpad[r0 + pad - 3 + j:r0 + pad - 3 + j + rb, :]
            y = _silu(acc)
            if mode != "v":
                y = y * lax.rsqrt(jnp.sum(y * y, axis=-1, keepdims=True) + L2_EPS)
            if mode == "q":
                y = y * (hd ** -0.5)
            dst[r0:r0 + rb, :] = y

    lane = lax.broadcasted_iota(jnp.int32, (rb, LANES), 1)
    for r0 in range(0, seq, rb):
        bg = bg_s[r0:r0 + rb, :]
        gcol = jnp.sum(jnp.where(lane == heads + head, bg, 0.0), axis=-1, keepdims=True)
        bcol = jnp.sum(jnp.where(lane == head, bg, 0.0), axis=-1, keepdims=True)
        gs[r0:r0 + rb, :] = jnp.broadcast_to(gcol, (rb, hd))
        bs[r0:r0 + rb, :] = jnp.broadcast_to(bcol, (rb, hd))


def _gdn_gate_terms(sm_ref, alog_ref, dtb_ref, bg_s, *, seq, heads):
    rb = GDN_GROUP
    lane = lax.broadcasted_iota(jnp.int32, (rb, LANES), 1)
    neg_a = -jnp.exp(alog_ref[...])
    dtb = dtb_ref[...]
    for r0 in range(0, seq, rb):
        s = sm_ref[r0:r0 + rb, :]
        xa = s + dtb
        sp = jnp.maximum(xa, 0.0) + jnp.log(1.0 + jnp.exp(-jnp.abs(xa)))
        bg_s[r0:r0 + rb, :] = jnp.where(lane < heads, jax.nn.sigmoid(s), neg_a * sp)


def _gdn_solve(hh, r0s, qs, ks, vs, gs, bs, kq_s, n_s, oc_s, gl_s, consts, *, hd):
    rb = GDN_GROUP
    ch = GDN_CHUNK
    causal_ch, strict_ch, cum_m, eye_f, m16, m32_16, m64_32 = consts
    reps = rb // hd
    cpg = rb // ch
    gi = range(len(r0s))

    q = [qs[r0:r0 + rb, :] for r0 in r0s]
    k = [ks[r0:r0 + rb, :] for r0 in r0s]
    v = [vs[r0:r0 + rb, :] for r0 in r0s]
    bb = [bs[r0:r0 + rb, :] for r0 in r0s]
    gcl = [_dot01(cum_m, gs[r0:r0 + rb, :]) for r0 in r0s]
    gc = [t[0:rb, :] for t in gcl]
    gl = [t[rb:2 * rb, :] for t in gcl]
    eg = [jnp.exp(t) for t in gc]
    gcw = [t if reps == 1 else jnp.concatenate([t] * reps, axis=1) for t in gc]
    e = [jnp.exp(jnp.where(causal_ch, t - t.T, 0.0)) for t in gcw]
    kb = [k[i] * bb[i] for i in gi]
    kkqk = [lax.dot_general(jnp.concatenate([kb[i], q[i]], axis=0).astype(BF16), k[i].astype(BF16),
                            (((1,), (1,)), ((), ())), preferred_element_type=F32) for i in gi]
    a = [jnp.where(strict_ch, kkqk[i][0:rb, :] * e[i], 0.0) for i in gi]
    attn = [jnp.where(causal_ch, kkqk[i][rb:2 * rb, :] * e[i], 0.0).astype(BF16) for i in gi]
    p = [-(t * m16) for t in a]
    tinv = [eye_f + t for t in p]
    for _ in range(3):
        p = [_bdot(t, t) for t in p]
        tinv = [tinv[i] + _bdot(tinv[i], p[i]) for i in gi]
    for msk in (m32_16, m64_32):
        low = [_bdot(a[i] * msk, tinv[i]) for i in gi]
        tinv = [tinv[i] - _bdot(tinv[i], low[i]) for i in gi]
    wu = [_bdot(tinv[i], jnp.concatenate([kb[i] * eg[i], v[i] * bb[i]], axis=1)).astype(BF16) for i in gi]
    aw = [jnp.dot(attn[i], wu[i], preferred_element_type=F32) for i in gi]
    qe = [(q[i] * eg[i] - aw[i][:, 0:hd]).astype(BF16) for i in gi]
    kd = [(k[i] * jnp.exp(gl[i] - gc[i])).astype(BF16) for i in gi]
    egl = [jnp.exp(t) for t in gl]
    for i, r0 in enumerate(r0s):
        oc_s[hh, r0:r0 + rb, :] = aw[i][:, hd:2 * hd]
    for c in range(cpg):
        rows = slice(c * ch, (c + 1) * ch)
        kn = [lax.dot_general(kd[i][rows, :], wu[i][rows, :], (((0,), (0,)), ((), ())),
                              preferred_element_type=F32) for i in gi]
        for i, r0 in enumerate(r0s):
            n = r0 // ch + c
            kq_s[hh, n, 0:hd, :] = kn[i][:, 0:hd].astype(BF16)
            kq_s[hh, n, hd:hd + ch, :] = qe[i][rows, :]
            n_s[hh, n] = kn[i][:, hd:2 * hd]
            gl_s[hh, n] = egl[i][c * ch:c * ch + SUBLANES, :]


def _gdn_body(l_ref, q_ref, k_ref, v_ref, z_ref, sm_ref, cwq_ref, cwk_ref, cwv_ref, alog_ref, dtb_ref,
              nw_ref, o_ref, xpad, bg_s, qs, ks, vs, gs, bs, kq_s, n_s, oc_s, gl_s, *, seq, heads, hd, hp):
    rb = GDN_GROUP
    ch = GDN_CHUNK
    head0 = pl.program_id(1) * hp

    xpad[0:SUBLANES, :] = jnp.zeros((SUBLANES, hd), F32)
    _gdn_gate_terms(sm_ref, alog_ref, dtb_ref, bg_s, seq=seq, heads=heads)
    row = lax.broadcasted_iota(jnp.int32, (rb, rb), 0)
    col = lax.broadcasted_iota(jnp.int32, (rb, rb), 1)

    def same(blk):
        return (row // blk) == (col // blk)

    def as_f(m):
        return jnp.where(m, 1.0, 0.0).astype(F32)

    same_ch = same(ch)
    causal_ch = same_ch & (row >= col)
    strict_ch = same_ch & (row > col)
    cum_m = jnp.concatenate([as_f(causal_ch), as_f(same_ch)], axis=0).astype(BF16)
    m16 = as_f(same(16))
    m32 = as_f(same(32))
    consts = (causal_ch, strict_ch, cum_m, as_f(row == col), m16, m32 - m16, as_f(same_ch) - m32)
    step_rows = rb * GDN_INTERLEAVE
    for hh in range(hp):
        _gdn_prepare(hh, head0 + hh, (q_ref, k_ref, v_ref), (cwq_ref, cwk_ref, cwv_ref), bg_s, xpad, qs, ks, vs,
                     gs, bs, seq=seq, heads=heads, hd=hd)
        for base in range(0, seq, step_rows):
            r0s = list(range(base, min(base + step_rows, seq), rb))
            _gdn_solve(hh, r0s, qs, ks, vs, gs, bs, kq_s, n_s, oc_s, gl_s, consts, hd=hd)

    nw = nw_ref[...]

    def step(n, states):
        r0 = pl.multiple_of(n * ch, ch)
        new = []
        for hh in range(hp):
            state = states[hh]
            r = jnp.dot(kq_s[hh, n], state.astype(BF16), preferred_element_type=F32)
            out = r[hd:hd + ch, :] + oc_s[hh, pl.ds(r0, ch), :]
            new.append(state * gl_s[hh, n][0:1, :] + (n_s[hh, n] - r[0:hd, :]))
            o = out * lax.rsqrt(jnp.mean(out * out, axis=-1, keepdims=True) + RMS_EPS) * nw
            zz = z_ref[pl.ds(r0, ch), hh * hd:(hh + 1) * hd].astype(F32)
            o_ref[pl.ds(r0, ch), hh * hd:(hh + 1) * hd] = (o * _silu(zz)).astype(o_ref.dtype)
        return tuple(new)

    lax.fori_loop(0, seq // ch, step, tuple(jnp.zeros((hd, hd), F32) for _ in range(hp)), unroll=2)


def _gdn(l, proj, small, conv_w, alog, dtb, norm_w, *, batch, seq, heads, hd, col0):
    m = batch * seq
    hp = GDN_HEADS_PER_STEP
    cb = col0 // hp
    hb = heads // hp
    nch = seq // GDN_CHUNK

    def act_spec(off):
        return pl.BlockSpec((seq, hp * hd), lambda b, h, l, off=off: (b, off + h))

    def cw_spec(off):
        return pl.BlockSpec((None, CONV_K, hp * hd), lambda b, h, l, off=off: (l[0], 0, off + h))

    row_spec = pl.BlockSpec((None, 1, LANES), lambda b, h, l: (l[0], 0, 0))
    return pl.pallas_call(
        functools.partial(_gdn_body, seq=seq, heads=heads, hd=hd, hp=hp),
        grid_spec=pltpu.PrefetchScalarGridSpec(
            num_scalar_prefetch=1,
            grid=(batch, hb),
            in_specs=[
                act_spec(cb), act_spec(cb + hb), act_spec(cb + 2 * hb), act_spec(cb + 3 * hb),
                pl.BlockSpec((seq, LANES), lambda b, h, l: (b, 0), pipeline_mode=pl.Buffered(1)),
                cw_spec(0), cw_spec(hb), cw_spec(2 * hb),
                row_spec, row_spec,
                pl.BlockSpec((None, 1, hd), lambda b, h, l: (l[0], 0, 0)),
            ],
            out_specs=pl.BlockSpec((seq, hp * hd), lambda b, h, l: (b, h)),
            scratch_shapes=(
                [pltpu.VMEM((seq + SUBLANES, hd), F32), pltpu.VMEM((seq, LANES), F32)]
                + [pltpu.VMEM((seq, hd), F32) for _ in range(5)]
                + [pltpu.VMEM((hp, nch, hd + GDN_CHUNK, hd), BF16),
                   pltpu.VMEM((hp, nch, hd, hd), F32),
                   pltpu.VMEM((hp, seq, hd), F32),
                   pltpu.VMEM((hp, nch, SUBLANES, hd), F32)]
            ),
        ),
        out_shape=jax.ShapeDtypeStruct((m, heads * hd), ACT),
        compiler_params=_cparams(("parallel", "parallel")),
        name="gdn",
    )(l, proj, proj, proj, proj, small, conv_w, conv_w, conv_w, alog, dtb, norm_w)


def _merge_body(l_ref, x_ref, ys_ref, yg_ref, gs0_ref, gs1_ref, gd0_ref, gd1_ref, wbs_ref, wbg_ref, wo_ref,
                g_ref, b_ref, o_ref, *, alpha):
    ys = ys_ref[...].astype(BF16)
    yg = yg_ref[...].astype(BF16)
    hw = wo_ref.shape[0] // 2
    acc = None
    for half, (gsr, gdr) in enumerate(((gs0_ref, gd0_ref), (gs1_ref, gd1_ref))):
        sl = slice(half * hw, (half + 1) * hw)
        mix = (jax.nn.sigmoid(gsr[...].astype(F32)) * jnp.dot(ys, wbs_ref[:, sl], preferred_element_type=F32)
               + jax.nn.sigmoid(gdr[...].astype(F32)) * jnp.dot(yg, wbg_ref[:, sl], preferred_element_type=F32))
        part = jnp.dot(mix.astype(BF16), wo_ref[sl, :], preferred_element_type=F32)
        acc = part if acc is None else acc + part
    y = alpha * x_ref[...] + acc
    o_ref[...] = _layer_norm(y, g_ref[...], b_ref[...])


def _merge(l, x, y_ssm, y_gdn, proj, wbs, wbg, wo, g, b, *, gate_col0, alpha):
    m, d = x.shape
    hw = d // 2
    ws = y_ssm.shape[1]
    wg = y_gdn.shape[1]
    tm = _pick(m, (512, 256, 128))

    def gate_spec(off):
        return pl.BlockSpec((tm, hw), lambda i, l, off=off: (i, gate_col0 + off))

    return pl.pallas_call(
        functools.partial(_merge_body, alpha=alpha),
        grid_spec=pltpu.PrefetchScalarGridSpec(
            num_scalar_prefetch=1,
            grid=(m // tm,),
            in_specs=[
                pl.BlockSpec((tm, d), lambda i, l: (i, 0)),
                pl.BlockSpec((tm, ws), lambda i, l: (i, 0)),
                pl.BlockSpec((tm, wg), lambda i, l: (i, 0)),
                gate_spec(0), gate_spec(1), gate_spec(2), gate_spec(3),
                pl.BlockSpec((None, ws, d), lambda i, l: (l[0], 0, 0), pipeline_mode=pl.Buffered(1)),
                pl.BlockSpec((None, wg, d), lambda i, l: (l[0], 0, 0), pipeline_mode=pl.Buffered(1)),
                pl.BlockSpec((None, d, d), lambda i, l: (l[0], 0, 0), pipeline_mode=pl.Buffered(1)),
                pl.BlockSpec((None, 1, d), lambda i, l: (l[0], 0, 0)),
                pl.BlockSpec((None, 1, d), lambda i, l: (l[0], 0, 0)),
            ],
            out_specs=pl.BlockSpec((tm, d), lambda i, l: (i, 0)),
        ),
        out_shape=jax.ShapeDtypeStruct((m, d), F32),
        compiler_params=_cparams(("parallel",)),
        name="merge",
    )(l, x, y_ssm, y_gdn, proj, proj, proj, proj, wbs, wbg, wo, g, b)


def kernel(x, ffn1_w_gu, ffn1_w_down, ln1_g, ln1_b, w_in, conv_w, ssm_a_re, ssm_a_im, ssm_log_dt, ssm_b_re,
           ssm_b_im, ssm_c_re, ssm_c_im, ssm_d, glu_w, glu_b, gdn_a_log, gdn_dt_bias, gdn_norm_w, w_br_ssm,
           w_br_gdn, w_out, ln2_g, ln2_b, ffn2_w_gu, ffn2_w_down, ln3_g, ln3_b):
    batch, seq, d = x.shape
    depth = ffn1_w_gu.shape[0]
    alpha = (2.0 * depth) ** 0.25
    groups, state, sgrp = ssm_b_re.shape[1:]
    ssm_width = groups * sgrp
    heads = gdn_a_log.shape[1]
    hd = gdn_norm_w.shape[1]
    gdn_width = heads * hd
    n_act = ssm_width + 4 * gdn_width
    n_small = 2 * heads

    def row3(t):
        return t.reshape(depth, 1, -1).astype(F32)

    w_act = w_in[:, :, :n_act].astype(BF16)
    w_gate = w_in[:, :, n_act + n_small:].astype(BF16)
    w_small = jnp.pad(w_in[:, :, n_act:n_act + n_small], ((0, 0), (0, 0), (0, LANES - n_small))).astype(BF16)
    wbs, wbg, wo = w_br_ssm.astype(BF16), w_br_gdn.astype(BF16), w_out.astype(BF16)
    gluw = glu_w.astype(BF16)

    gpb = S5_GROUPS_PER_BLOCK
    nb = groups // gpb

    def b_stack(t):
        return jnp.swapaxes(t, 2, 3).reshape(depth, nb, gpb * sgrp, state).astype(F32)

    def c_stack(t):
        return jnp.swapaxes(t, 2, 3).reshape(depth, nb, gpb * state, sgrp).astype(F32)

    def blk_rows(t):
        return t.reshape(depth, nb, 1, gpb * state).astype(F32)

    bbar, cre_bd, cim_bd, lam = _s5_prep(blk_rows(ssm_a_re), blk_rows(ssm_a_im),
                                         blk_rows(jnp.repeat(ssm_log_dt, state, axis=1)),
                                         b_stack(ssm_b_re), b_stack(ssm_b_im), c_stack(ssm_c_re),
                                         c_stack(ssm_c_im), batch=batch, gpb=gpb)

    alog = jnp.pad(gdn_a_log, ((0, 0), (heads, LANES - 2 * heads))).reshape(depth, 1, LANES)
    dtb = jnp.pad(gdn_dt_bias, ((0, 0), (heads, LANES - 2 * heads))).reshape(depth, 1, LANES)

    def layer(xc, li_):
        l = jnp.reshape(li_, (1,)).astype(jnp.int32)
        xc = _ffn(l, xc, ffn1_w_gu, ffn1_w_down, row3(ln1_g), row3(ln1_b), alpha=alpha)
        proj, small = _inproj(l, xc, w_act, w_gate, w_small)
        y_ssm = _s5(l, proj.reshape(batch, seq, -1), bbar, lam, cre_bd, cim_bd, row3(ssm_d), gluw, row3(glu_b))
        y_ssm = y_ssm.reshape(batch * seq, ssm_width)
        y_gdn = _gdn(l, proj, small, conv_w, alog, dtb, row3(gdn_norm_w), batch=batch, seq=seq, heads=heads,
                     hd=hd, col0=ssm_width // hd)
        xc = _merge(l, xc, y_ssm, y_gdn, proj, wbs, wbg, wo, row3(ln2_g), row3(ln2_b),
                    gate_col0=n_act // (d // 2), alpha=alpha)
        xc = _ffn(l, xc, ffn2_w_gu, ffn2_w_down, row3(ln3_g), row3(ln3_b), alpha=alpha)
        return xc, None

    out, _ = lax.scan(layer, x.reshape(batch * seq, d), jnp.arange(depth, dtype=jnp.int32))
    return out.reshape(batch, seq, d)
```
